```python
import math
import jax, jax.numpy as jnp
from jax import lax
import numpy as np


D_MODEL = 1024
BATCH = 4
SEQ = 8192
DEPTH = 1

PLE_DIM = 256
DN_HEADS = 4
DN_HEAD_DIM = 128
DN_WIDTH = DN_HEADS * DN_HEAD_DIM
CONV_WIDTH = 4
CHUNK = 64
DA_HEADS = 4
DA_HEAD_DIM = 64
DA_V_DIM = 2 * DA_HEAD_DIM
DA_QK_WIDTH = DA_HEADS * 2 * DA_HEAD_DIM
DA_WIDTH = DA_HEADS * DA_V_DIM
Q_BLOCK = 128
N_GROUPS = 4
EXPERTS_PER_GROUP = 4
N_EXPERTS = N_GROUPS * EXPERTS_PER_GROUP
TOP_K_IN_GROUP = 2
EXPERT_FF = 256
IN_SIZES = (3 * DN_WIDTH, DN_WIDTH, DN_HEADS, DN_HEADS, DA_QK_WIDTH, DA_QK_WIDTH, DA_WIDTH, 2 * D_MODEL)
IN_DIM = sum(IN_SIZES)
LN_EPS = 1e-5
RMS_EPS = 1e-6

kernel_name = 'hybrid_deltanet_diffattn_hmoe_deepnorm'


def layer_norm(x, g, b):
    xf = x.astype(jnp.float32)
    mu = jnp.mean(xf, axis=-1, keepdims=True)
    var = jnp.mean(jnp.square(xf - mu), axis=-1, keepdims=True)
    return ((xf - mu) * lax.rsqrt(var + LN_EPS) * g + b).astype(x.dtype)


def rms_norm(x, w):
    xf = x.astype(jnp.float32)
    return (xf * lax.rsqrt(jnp.mean(jnp.square(xf), axis=-1, keepdims=True) + RMS_EPS) * w).astype(x.dtype)


def l2_normalize(x):
    return x * lax.rsqrt(jnp.sum(jnp.square(x), axis=-1, keepdims=True) + RMS_EPS)


def causal_conv_silu(u, w):
    S = u.shape[1]
    K = w.shape[0]
    up = jnp.pad(u, ((0, 0), (K - 1, 0), (0, 0)))
    out = up[:, 0:S] * w[0]
    for j in range(1, K):
        out = out + up[:, j:j + S] * w[j]
    return jax.nn.silu(out)


def gated_delta_rule(q, k, v, beta, g):
    out_dtype = v.dtype
    f32 = jnp.float32
    B, S, H, dk = q.shape
    dv = v.shape[-1]
    n = S // CHUNK
    q = l2_normalize(q.astype(f32)) * (dk ** -0.5)
    k = l2_normalize(k.astype(f32))
    v = v.astype(f32)
    to_chunks = lambda t: jnp.moveaxis(t.reshape(B, n, CHUNK, H, -1), 3, 1)
    q, k, v = to_chunks(q), to_chunks(k), to_chunks(v)
    beta = jnp.moveaxis(beta.astype(f32).reshape(B, n, CHUNK, H), 3, 1)
    g = jnp.cumsum(jnp.moveaxis(g.astype(f32).reshape(B, n, CHUNK, H), 3, 1), axis=-1)
    idx = jnp.arange(CHUNK)
    causal = idx[:, None] >= idx[None, :]
    strict = idx[:, None] > idx[None, :]
    decay = jnp.exp(jnp.where(causal, g[..., :, None] - g[..., None, :], -jnp.inf))
    kb = k * beta[..., None]
    nmat = jnp.where(strict, -jnp.einsum('bhnid,bhnjd->bhnij', kb, k) * decay, 0.0)
    eye = jnp.eye(CHUNK, dtype=f32)
    T = eye + nmat
    power = nmat
    for _ in range(int(math.log2(CHUNK)) - 1):
        power = power @ power
        T = T @ (eye + power)
    u = T @ (v * beta[..., None])
    w = T @ (kb * jnp.exp(g)[..., None])
    a_intra = jnp.where(causal, jnp.einsum('bhnid,bhnjd->bhnij', q, k) * decay, 0.0)
    qg = q * jnp.exp(g)[..., None]
    g_last = g[..., -1]
    kg = k * jnp.exp(g_last[..., None] - g)[..., None]

    def step(state, xs):
        u_c, w_c, qg_c, kg_c, a_c, gl_c = xs
        v_new = u_c - w_c @ state
        o_c = qg_c @ state + a_c @ v_new
        state = state * jnp.exp(gl_c)[..., None, None] + jnp.swapaxes(kg_c, -1, -2) @ v_new
        return state, o_c

    xs = tuple(jnp.moveaxis(t, 2, 0) for t in (u, w, qg, kg, a_intra, g_last))
    state0 = jnp.zeros((B, H, dk, dv), f32)
    _, o = lax.scan(step, state0, xs)
    return jnp.transpose(o, (1, 0, 3, 2, 4)).reshape(B, S, H, dv).astype(out_dtype)


def diff_attention_alibi(q, k, v, lam):
    f32 = jnp.float32
    B, S, H, _, d = q.shape
    nb = S // Q_BLOCK
    slopes = jnp.asarray(2.0 ** (-8.0 * np.arange(1, H + 1) / H), dtype=f32)
    qb = jnp.moveaxis((q * (d ** -0.5)).reshape(B, nb, Q_BLOCK, H, 2, d), 1, 0)
    kpos = jnp.arange(S)

    def block(args):
        qi, bi = args
        qpos = bi * Q_BLOCK + jnp.arange(Q_BLOCK)
        dist = qpos[:, None] - kpos[None, :]
        s = jnp.einsum('bqhmd,bkhmd->bhmqk', qi, k, preferred_element_type=f32)
        s = s - slopes[:, None, None, None] * dist.astype(f32)
        s = jnp.where(dist >= 0, s, -jnp.inf)
        pr = jax.nn.softmax(s, axis=-1)
        a = pr[:, :, 0] - lam * pr[:, :, 1]
        return jnp.einsum('bhqk,bkhe->bqhe', a.astype(v.dtype), v,
                          preferred_element_type=f32).astype(v.dtype)

    o = lax.map(block, (qb, jnp.arange(nb)))
    return jnp.moveaxis(o, 0, 1).reshape(B, S, H, v.shape[-1])


def hierarchical_moe(h, w_rg, b_rg, w_re, b_re, w_gate, w_up, w_down):
    f32 = jnp.float32
    group_prob = jax.nn.softmax((h @ w_rg + b_rg).astype(f32), axis=-1)
    g_val, g_idx = lax.top_k(group_prob, 1)
    exp_logits = (h @ w_re + b_re).astype(f32)
    exp_logits = exp_logits.reshape(h.shape[:-1] + (N_GROUPS, EXPERTS_PER_GROUP))
    in_group = jnp.einsum('bsge,bsg->bse', exp_logits, jax.nn.one_hot(g_idx[..., 0], N_GROUPS, dtype=f32))
    e_prob = jax.nn.softmax(in_group, axis=-1)
    e_val, e_idx = lax.top_k(e_prob, TOP_K_IN_GROUP)
    e_w = e_val / jnp.sum(e_val, axis=-1, keepdims=True) * g_val
    global_idx = g_idx * EXPERTS_PER_GROUP + e_idx
    combine = jnp.einsum('bsk,bske->bse', e_w, jax.nn.one_hot(global_idx, N_EXPERTS, dtype=f32))
    hid = jax.nn.silu(jnp.einsum('bsd,edf->bsef', h, w_gate)) * jnp.einsum('bsd,edf->bsef', h, w_up)
    return jnp.einsum('bsef,efd->bsd', hid * combine[..., None].astype(h.dtype), w_down)


def setup_inputs(seed: int = 0) -> dict:
    key = jax.random.key(seed)
    ks = jax.random.split(key, 40)
    f32 = jnp.float32
    nrm = lambda kk, shape, scale: jax.random.normal(kk, shape, f32) * scale
    L = DEPTH
    dn_beta = (8.0 * DEPTH) ** -0.25
    dt = jnp.exp(jax.random.uniform(ks[8], (L, DN_HEADS), f32, math.log(1e-3), math.log(1e-1)))
    return {
        'x': nrm(ks[0], (BATCH, SEQ, D_MODEL), 1.0),
        'p': nrm(ks[1], (DEPTH, BATCH, SEQ, PLE_DIM), 1.0),
        'emb_ln_g': 1.0 + nrm(ks[2], (D_MODEL,), 0.02),
        'emb_ln_b': nrm(ks[3], (D_MODEL,), 0.02),
        'w_in': nrm(ks[4], (L, D_MODEL, IN_DIM), D_MODEL ** -0.5),
        'b_in': nrm(ks[5], (L, IN_DIM), 0.02),
        'conv_w': nrm(ks[6], (L, CONV_WIDTH, 3 * DN_WIDTH), CONV_WIDTH ** -0.5),
        'dn_a_log': jnp.log(jax.random.uniform(ks[7], (L, DN_HEADS), f32, 1.0, 16.0)),
        'dn_dt_bias': dt + jnp.log(-jnp.expm1(-dt)),
        'dn_norm_w': 1.0 + nrm(ks[9], (L, DN_HEAD_DIM), 0.02),
        'w_dn_o': nrm(ks[10], (L, DN_WIDTH, D_MODEL), DN_WIDTH ** -0.5 * dn_beta),
        'da_lq1': nrm(ks[11], (L, DA_HEAD_DIM), 0.1),
        'da_lk1': nrm(ks[12], (L, DA_HEAD_DIM), 0.1),
        'da_lq2': nrm(ks[13], (L, DA_HEAD_DIM), 0.1),
        'da_lk2': nrm(ks[14], (L, DA_HEAD_DIM), 0.1),
        'da_subln_w': 1.0 + nrm(ks[15], (L, DA_V_DIM), 0.02),
        'w_da_o': nrm(ks[16], (L, DA_WIDTH, D_MODEL), DA_WIDTH ** -0.5 * dn_beta),
        'w_out': nrm(ks[17], (L, D_MODEL, D_MODEL), D_MODEL ** -0.5 * dn_beta),
        'ln1_g': 1.0 + nrm(ks[18], (L, D_MODEL), 0.02),
        'ln1_b': nrm(ks[19], (L, D_MODEL), 0.02),
        'w_router_group': nrm(ks[20], (L, D_MODEL, N_GROUPS), D_MODEL ** -0.5),
        'b_router_group': nrm(ks[21], (L, N_GROUPS), 0.01),
        'w_router_expert': nrm(ks[22], (L, D_MODEL, N_EXPERTS), D_MODEL ** -0.5),
        'b_router_expert': nrm(ks[23], (L, N_EXPERTS), 0.01),
        'w_exp_gate': nrm(ks[24], (L, N_EXPERTS, D_MODEL, EXPERT_FF), D_MODEL ** -0.5),
        'w_exp_up': nrm(ks[25], (L, N_EXPERTS, D_MODEL, EXPERT_FF), D_MODEL ** -0.5),
        'w_exp_down': nrm(ks[26], (L, N_EXPERTS, EXPERT_FF, D_MODEL), EXPERT_FF ** -0.5 * dn_beta),
        'w_ple_gate': nrm(ks[27], (L, D_MODEL, D_MODEL), D_MODEL ** -0.5),
        'b_ple_gate': nrm(ks[28], (L, D_MODEL), 0.02),
        'w_ple_proj': nrm(ks[29], (L, PLE_DIM, D_MODEL), PLE_DIM ** -0.5 * dn_beta),
        'ln2_g': 1.0 + nrm(ks[30], (L, D_MODEL), 0.02),
        'ln2_b': nrm(ks[31], (L, D_MODEL), 0.02),
    }


def reference(x, p, emb_ln_g, emb_ln_b, w_in, b_in, conv_w, dn_a_log, dn_dt_bias, dn_norm_w, w_dn_o,
              da_lq1, da_lk1, da_lq2, da_lk2, da_subln_w, w_da_o, w_out, ln1_g, ln1_b,
              w_router_group, b_router_group, w_router_expert, b_router_expert,
              w_exp_gate, w_exp_up, w_exp_down, w_ple_gate, b_ple_gate, w_ple_proj, ln2_g, ln2_b):
    f32 = jnp.float32
    alpha = (2.0 * DEPTH) ** 0.25
    B, S, _ = x.shape
    splits = [int(c) for c in np.cumsum(IN_SIZES)[:-1]]
    h = layer_norm(x, emb_ln_g, emb_ln_b)
    for i in range(DEPTH):
        proj = h @ w_in[i] + b_in[i]
        qkv_dn, z_dn, b_dn, a_dn, q_da, k_da, v_da, gate_logits = jnp.split(proj, splits, axis=-1)
        qkv_dn = causal_conv_silu(qkv_dn, conv_w[i])
        q_dn, k_dn, v_dn = jnp.split(qkv_dn, 3, axis=-1)
        beta_dn = jax.nn.sigmoid(b_dn.astype(f32))
        g_dn = -jnp.exp(dn_a_log[i].astype(f32)) * jax.nn.softplus(a_dn.astype(f32) + dn_dt_bias[i])
        o_dn = gated_delta_rule(q_dn.reshape(B, S, DN_HEADS, DN_HEAD_DIM),
                                k_dn.reshape(B, S, DN_HEADS, DN_HEAD_DIM),
                                v_dn.reshape(B, S, DN_HEADS, DN_HEAD_DIM), beta_dn, g_dn)
        o_dn = rms_norm(o_dn, dn_norm_w[i]) * jax.nn.silu(z_dn.reshape(B, S, DN_HEADS, DN_HEAD_DIM))
        y_dn = o_dn.reshape(B, S, DN_WIDTH) @ w_dn_o[i]
        lam_init = 0.8 - 0.6 * math.exp(-0.3 * i)
        lam = (jnp.exp(jnp.sum(da_lq1[i].astype(f32) * da_lk1[i].astype(f32)))
               - jnp.exp(jnp.sum(da_lq2[i].astype(f32) * da_lk2[i].astype(f32))) + lam_init)
        o_da = diff_attention_alibi(q_da.reshape(B, S, DA_HEADS, 2, DA_HEAD_DIM),
                                    k_da.reshape(B, S, DA_HEADS, 2, DA_HEAD_DIM),
                                    v_da.reshape(B, S, DA_HEADS, DA_V_DIM), lam)
        o_da = rms_norm(o_da, da_subln_w[i]) * (1.0 - lam_init)
        y_da = o_da.reshape(B, S, DA_WIDTH) @ w_da_o[i]
        gate_dn, gate_da = jnp.split(jax.nn.sigmoid(gate_logits), 2, axis=-1)
        mix = (gate_dn * y_dn + gate_da * y_da) @ w_out[i]
        h = layer_norm(alpha * h + mix, ln1_g[i], ln1_b[i])
        ffn = hierarchical_moe(h, w_router_group[i], b_router_group[i], w_router_expert[i],
                               b_router_expert[i], w_exp_gate[i], w_exp_up[i], w_exp_down[i])
        ple = jax.nn.sigmoid(h @ w_ple_gate[i] + b_ple_gate[i]) * (p[i] @ w_ple_proj[i])
        h = layer_norm(alpha * h + ffn + ple, ln2_g[i], ln2_b[i])
    return h
```

```python
import functools
import math

import jax
import jax.numpy as jnp
import numpy as np
from jax import lax
from jax.experimental import pallas as pl
from jax.experimental.pallas import tpu as pltpu

F32 = jnp.float32
BF16 = jnp.bfloat16

D_MODEL = 1024
PLE_DIM = 256
DN_HEADS = 4
DN_HEAD_DIM = 128
DN_WIDTH = DN_HEADS * DN_HEAD_DIM
CONV_WIDTH = 4
CHUNK = 64
DA_HEADS = 4
DA_HEAD_DIM = 64
DA_V_DIM = 2 * DA_HEAD_DIM
DA_QK_WIDTH = DA_HEADS * 2 * DA_HEAD_DIM
DA_WIDTH = DA_HEADS * DA_V_DIM
N_GROUPS = 4
EXPERTS_PER_GROUP = 4
N_EXPERTS = N_GROUPS * EXPERTS_PER_GROUP
EXPERT_FF = 256
IN_SIZES = (3 * DN_WIDTH, DN_WIDTH, DN_HEADS, DN_HEADS, DA_QK_WIDTH, DA_QK_WIDTH, DA_WIDTH, 2 * D_MODEL)
LN_EPS = 1e-5
RMS_EPS = 1e-6
DEPTH = 1
ALPHA = (2.0 * DEPTH) ** 0.25
LAM_INIT = 0.8 - 0.6 * math.exp(-0.3 * 0)

LANES = 128
VMEM_LIMIT = 56 * 1024 * 1024

NEG_INF = float("-inf")


def _cparams(sem):
    return pltpu.CompilerParams(dimension_semantics=sem, vmem_limit_bytes=VMEM_LIMIT)


def _layer_norm(x, g, b):
    mu = jnp.mean(x, axis=-1, keepdims=True)
    xc = x - mu
    var = jnp.mean(xc * xc, axis=-1, keepdims=True)
    return xc * lax.rsqrt(var + LN_EPS) * g + b


def _sigmoid(x):
    return 1.0 / (1.0 + jnp.exp(-x))


def _silu(x):
    return x * _sigmoid(x)


def _dot(a, b):
    return jnp.dot(a, b, preferred_element_type=F32)


def _dot_nt(a, b):
    return lax.dot_general(a, b, (((1,), (1,)), ((), ())), preferred_element_type=F32)


def _dot_tn(a, b):
    return lax.dot_general(a, b, (((0,), (0,)), ((), ())), preferred_element_type=F32)


DN_COLS = 3 * DN_WIDTH
DA_COLS = 2 * DA_QK_WIDTH + DA_WIDTH


def _in_proj_kernel(x_ref, g_ref, b_ref, w_ref, bias_ref, alog_ref, dtb_ref, dn_ref, ba_ref, da_ref):
    h = _layer_norm(x_ref[...], g_ref[...], b_ref[...])
    hb = h.astype(BF16)
    dn_ref[...] = _dot(hb, w_ref[:, :DN_COLS]) + bias_ref[:, :DN_COLS]
    da_ref[...] = (_dot(hb, w_ref[:, DN_COLS + LANES:]) + bias_ref[:, DN_COLS + LANES:]).astype(BF16)
    ba = _dot(hb, w_ref[:, DN_COLS:DN_COLS + LANES]) + bias_ref[:, DN_COLS:DN_COLS + LANES]
    tm = ba.shape[0]
    lane = lax.broadcasted_iota(jnp.int32, (tm, LANES), 1)
    row = lax.broadcasted_iota(jnp.int32, (tm, LANES), 0)
    beta = _sigmoid(ba)
    xs = ba + dtb_ref[...]
    softplus = jnp.maximum(xs, 0.0) + jnp.log(1.0 + jnp.exp(-jnp.abs(xs)))
    g = -jnp.exp(alog_ref[...]) * softplus
    g = jnp.where((lane >= DN_HEADS) & (lane < 2 * DN_HEADS), g, 0.0)
    rin = row & (CHUNK - 1)
    k = 1
    while k < CHUNK:
        g = g + jnp.where(rin >= k, pltpu.roll(g, k, 0), 0.0)
        k *= 2
    ba_ref[...] = jnp.where(lane < DN_HEADS, beta, g)


def _in_proj(x2, emb_g, emb_b, w_a, b_a, alog, dtb, tm):
    n = x2.shape[0]
    wcols = w_a.shape[1]
    const = lambda i: (0, 0)
    return pl.pallas_call(
        _in_proj_kernel,
        grid=(n // tm,),
        in_specs=[
            pl.BlockSpec((tm, D_MODEL), lambda i: (i, 0)),
            pl.BlockSpec((1, D_MODEL), const),
            pl.BlockSpec((1, D_MODEL), const),
            pl.BlockSpec((D_MODEL, wcols), const),
            pl.BlockSpec((1, wcols), const),
            pl.BlockSpec((1, LANES), const),
            pl.BlockSpec((1, LANES), const),
        ],
        out_specs=[
            pl.BlockSpec((tm, DN_COLS), lambda i: (i, 0)),
            pl.BlockSpec((tm, LANES), lambda i: (i, 0)),
            pl.BlockSpec((tm, DA_COLS), lambda i: (i, 0)),
        ],
        out_shape=[
            jax.ShapeDtypeStruct((n, DN_COLS), F32),
            jax.ShapeDtypeStruct((n, LANES), F32),
            jax.ShapeDtypeStruct((n, DA_COLS), BF16),
        ],
        compiler_params=_cparams(("parallel",)),
        name="in_proj",
    )(x2, emb_g, emb_b, w_a, b_a, alog, dtb)


TAIL = 8


def _dn_kernel(q_ref, k_ref, v_ref, ba_ref, gt_ref, cwq_ref, cwk_ref, cwv_ref, nw_ref, o_ref,
               state_ref, qe_ref, ke_ref, ve_ref):
    hd = pl.program_id(1)
    sblk = pl.program_id(2)
    tb = q_ref.shape[0]

    @pl.when(sblk == 0)
    def _():
        state_ref[...] = jnp.zeros_like(state_ref)
        qe_ref[tb:tb + TAIL, :] = jnp.zeros((TAIL, LANES), F32)
        ke_ref[tb:tb + TAIL, :] = jnp.zeros((TAIL, LANES), F32)
        ve_ref[tb:tb + TAIL, :] = jnp.zeros((TAIL, LANES), F32)

    def conv_silu(x_ref, xe_ref, cw_ref):
        xe_ref[0:TAIL, :] = xe_ref[tb:tb + TAIL, :]
        xe_ref[TAIL:TAIL + tb, :] = x_ref[...]
        w = cw_ref[...]
        out = xe_ref[TAIL:TAIL + tb, :] * w[CONV_WIDTH - 1:CONV_WIDTH, :]
        for j in range(CONV_WIDTH - 1):
            off = TAIL - (CONV_WIDTH - 1) + j
            out = out + xe_ref[off:off + tb, :] * w[j:j + 1, :]
        return _silu(out)

    q = conv_silu(q_ref, qe_ref, cwq_ref)
    k = conv_silu(k_ref, ke_ref, cwk_ref)
    v = conv_silu(v_ref, ve_ref, cwv_ref)
    q = q * lax.rsqrt(jnp.sum(q * q, axis=-1, keepdims=True) + RMS_EPS) * (DN_HEAD_DIM ** -0.5)
    k = k * lax.rsqrt(jnp.sum(k * k, axis=-1, keepdims=True) + RMS_EPS)

    ba = ba_ref[...]
    lane = lax.broadcasted_iota(jnp.int32, ba.shape, 1)
    bcol = jnp.sum(jnp.where(lane == hd, ba, 0.0), axis=-1, keepdims=True)
    gcol = jnp.sum(jnp.where(lane == hd + DN_HEADS, ba, 0.0), axis=-1, keepdims=True)
    grow_all = gt_ref[...]
    egcol = jnp.exp(gcol)
    kb = k * bcol
    vb = v * bcol
    kbg = kb * egcol
    qg = q * egcol

    ri = lax.broadcasted_iota(jnp.int32, (CHUNK, CHUNK), 0)
    ci = lax.broadcasted_iota(jnp.int32, (CHUNK, CHUNK), 1)
    causal = ri >= ci
    strict = ri > ci
    eye = jnp.where(ri == ci, 1.0, 0.0).astype(F32)
    nw = nw_ref[...]

    state = state_ref[...]
    for c in range(tb // CHUNK):
        r0 = c * CHUNK
        sl = slice(r0, r0 + CHUNK)
        kc = k[sl]
        kc_b = kc.astype(BF16)
        gc = gcol[sl]
        gr = grow_all[:, r0:r0 + CHUNK]
        decay = jnp.exp(jnp.where(causal, gc - gr, NEG_INF))
        nmat = jnp.where(strict, -_dot_nt(kb[sl].astype(BF16), kc_b) * decay, 0.0)
        t = eye + nmat
        power = nmat
        for _ in range(int(math.log2(CHUNK)) - 1):
            pb = power.astype(BF16)
            power = _dot(pb, pb)
            t = t + _dot(t.astype(BF16), power.astype(BF16))
        tb16 = t.astype(BF16)
        u = _dot(tb16, vb[sl].astype(BF16))
        w = _dot(tb16, kbg[sl].astype(BF16))
        a_intra = jnp.where(causal, _dot_nt(q[sl].astype(BF16), kc_b) * decay, 0.0)
        g_last = gc[CHUNK - 1:CHUNK, :]
        kg = kc * jnp.exp(g_last - gc)
        sb = state.astype(BF16)
        v_new = u - _dot(w.astype(BF16), sb)
        o_c = _dot(qg[sl].astype(BF16), sb) + _dot(a_intra.astype(BF16), v_new.astype(BF16))
        state = state * jnp.exp(g_last) + _dot_tn(kg.astype(BF16), v_new.astype(BF16))
        o_n = o_c * lax.rsqrt(jnp.mean(o_c * o_c, axis=-1, keepdims=True) + RMS_EPS) * nw
        o_ref[sl, :] = o_n
    state_ref[...] = state


def _deltanet(dn3, ba3, gt4, conv_w, norm_w, tb):
    b, s, _ = dn3.shape
    hh = DN_HEADS
    return pl.pallas_call(
        _dn_kernel,
        grid=(b, hh, s // tb),
        in_specs=[
            pl.BlockSpec((None, tb, LANES), lambda bi, h, i: (bi, i, h)),
            pl.BlockSpec((None, tb, LANES), lambda bi, h, i: (bi, i, hh + h)),
            pl.BlockSpec((None, tb, LANES), lambda bi, h, i: (bi, i, 2 * hh + h)),
            pl.BlockSpec((None, tb, LANES), lambda bi, h, i: (bi, i, 0)),
            pl.BlockSpec((None, None, 1, tb), lambda bi, h, i: (bi, h, 0, i)),
            pl.BlockSpec((CONV_WIDTH, LANES), lambda bi, h, i: (0, h)),
            pl.BlockSpec((CONV_WIDTH, LANES), lambda bi, h, i: (0, hh + h)),
            pl.BlockSpec((CONV_WIDTH, LANES), lambda bi, h, i: (0, 2 * hh + h)),
            pl.BlockSpec((1, LANES), lambda bi, h, i: (0, 0)),
        ],
        out_specs=pl.BlockSpec((None, tb, LANES), lambda bi, h, i: (bi, i, h)),
        out_shape=jax.ShapeDtypeStruct((b, s, DN_WIDTH), F32),
        scratch_shapes=[
            pltpu.VMEM((DN_HEAD_DIM, DN_HEAD_DIM), F32),
            pltpu.VMEM((tb + 2 * TAIL, LANES), F32),
            pltpu.VMEM((tb + 2 * TAIL, LANES), F32),
            pltpu.VMEM((tb + 2 * TAIL, LANES), F32),
        ],
        compiler_params=_cparams(("parallel", "parallel", "arbitrary")),
        name="deltanet",
    )(dn3, dn3, dn3, ba3, gt4, conv_w, conv_w, conv_w, norm_w)


def _da_kernel(lq1_ref, lk1_ref, lq2_ref, lk2_ref, slope_ref, q_ref, k_ref, v_ref, w_ref, o_ref):
    i = pl.program_id(2)
    tq = q_ref.shape[0]
    tk = tq
    scale = DA_HEAD_DIM ** -0.5
    qv = q_ref[...]
    lane_q = lax.broadcasted_iota(jnp.int32, qv.shape, 1)
    zero = jnp.zeros_like(qv)
    qs = qv * jnp.asarray(scale, BF16)
    q1 = jnp.where(lane_q < DA_HEAD_DIM, qs, zero)
    q2 = jnp.where(lane_q >= DA_HEAD_DIM, qs, zero)
    slope = slope_ref[...]
    slope_k = jnp.concatenate([slope] * (tk // LANES), axis=1)
    col = lax.broadcasted_iota(jnp.int32, (1, tk), 1)
    rowi = lax.broadcasted_iota(jnp.int32, (tq, tk), 0)
    coli = lax.broadcasted_iota(jnp.int32, (tq, tk), 1)

    def update(s, m, l, acc, vb):
        m_new = jnp.maximum(m, jnp.max(s, axis=-1, keepdims=True))
        a = jnp.exp(m - m_new)
        p = jnp.exp(s - m_new)
        l = a * l + jnp.sum(p, axis=-1, keepdims=True)
        acc = a * acc + _dot(p.astype(BF16), vb)
        return m_new, l, acc

    def step(j, carry, masked):
        m1, l1, a1, m2, l2, a2 = carry
        start = pl.multiple_of(j * tk, tk)
        kb = k_ref[pl.ds(start, tk), :]
        vb = v_ref[pl.ds(start, tk), :]
        bias = slope_k * ((j - i) * tk + col).astype(F32)
        s1 = _dot_nt(q1, kb) + bias
        s2 = _dot_nt(q2, kb) + bias
        if masked:
            keep = coli <= rowi
            s1 = jnp.where(keep, s1, NEG_INF)
            s2 = jnp.where(keep, s2, NEG_INF)
        m1, l1, a1 = update(s1, m1, l1, a1, vb)
        m2, l2, a2 = update(s2, m2, l2, a2, vb)
        return m1, l1, a1, m2, l2, a2

    init_m = jnp.full((tq, 1), -1e30, F32)
    init_l = jnp.zeros((tq, 1), F32)
    init_a = jnp.zeros((tq, DA_V_DIM), F32)
    carry = (init_m, init_l, init_a, init_m, init_l, init_a)
    carry = lax.fori_loop(0, i, lambda j, c: step(j, c, False), carry)
    m1, l1, a1, m2, l2, a2 = step(i, carry, True)

    lam = (jnp.exp(jnp.sum(lq1_ref[...] * lk1_ref[...], axis=-1, keepdims=True))
           - jnp.exp(jnp.sum(lq2_ref[...] * lk2_ref[...], axis=-1, keepdims=True)) + LAM_INIT)
    o = a1 / l1 - lam * (a2 / l2)
    o = o * lax.rsqrt(jnp.mean(o * o, axis=-1, keepdims=True) + RMS_EPS) * w_ref[...] * (1.0 - LAM_INIT)
    o_ref[...] = o.astype(BF16)


def _diffattn(da3, lq1, lk1, lq2, lk2, slopes, subln_w, tq):
    b, s, _ = da3.shape
    hh = DA_HEADS
    c2 = lambda bi, h, i: (0, 0)
    return pl.pallas_call(
        _da_kernel,
        grid=(b, hh, s // tq),
        in_specs=[
            pl.BlockSpec((1, DA_HEAD_DIM), c2),
            pl.BlockSpec((1, DA_HEAD_DIM), c2),
            pl.BlockSpec((1, DA_HEAD_DIM), c2),
            pl.BlockSpec((1, DA_HEAD_DIM), c2),
            pl.BlockSpec((None, 1, LANES), lambda bi, h, i: (h, 0, 0)),
            pl.BlockSpec((None, tq, LANES), lambda bi, h, i: (bi, i, h)),
            pl.BlockSpec((None, s, LANES), lambda bi, h, i: (bi, 0, hh + h)),
            pl.BlockSpec((None, s, LANES), lambda bi, h, i: (bi, 0, 2 * hh + h)),
            pl.BlockSpec((1, DA_V_DIM), c2),
        ],
        out_specs=pl.BlockSpec((None, tq, LANES), lambda bi, h, i: (bi, i, h)),
        out_shape=jax.ShapeDtypeStruct((b, s, DA_WIDTH), BF16),
        compiler_params=_cparams(("parallel", "parallel", "arbitrary")),
        name="diffattn",
    )(lq1, lk1, lq2, lk2, slopes, da3, da3, da3, subln_w)


ROUTER_GROUP_LANE0 = N_EXPERTS


def _mixer_kernel(x_ref, odn_ref, oda_ref, p_ref, eg_ref, eb_ref, wz_ref, bz_ref, wg_ref, bg_ref,
                  wdn_ref, wda_ref, wout_ref, g1_ref, b1_ref, wr_hi_ref, wr_lo_ref, br_ref,
                  wpg_ref, bpg_ref, wpp_ref, h1b_ref, base_ref, comb_ref):
    h = _layer_norm(x_ref[...], eg_ref[...], eb_ref[...])
    hb = h.astype(BF16)
    z = _dot(hb, wz_ref[...]) + bz_ref[...]
    y_dn = _dot((odn_ref[...] * _silu(z)).astype(BF16), wdn_ref[...])
    y_da = _dot(oda_ref[...], wda_ref[...])
    gates = _sigmoid(_dot(hb, wg_ref[...]) + bg_ref[...])
    mixed = gates[:, :D_MODEL] * y_dn + gates[:, D_MODEL:] * y_da
    mix = _dot(mixed.astype(BF16), wout_ref[...])
    h1 = _layer_norm(ALPHA * h + mix, g1_ref[...], b1_ref[...])
    h1b = h1.astype(BF16)
    h1b_ref[...] = h1b
    ple = _sigmoid(_dot(h1b, wpg_ref[...]) + bpg_ref[...]) * _dot(p_ref[...].astype(BF16), wpp_ref[...])
    base_ref[...] = ALPHA * h1 + ple

    h1_lo = (h1 - h1b.astype(F32)).astype(BF16)
    logits = (_dot(h1b, wr_hi_ref[...]) + _dot(h1_lo, wr_hi_ref[...]) + _dot(h1b, wr_lo_ref[...])
              + br_ref[...])
    tm = logits.shape[0]
    lane = lax.broadcasted_iota(jnp.int32, (tm, LANES), 1).astype(F32)
    big = float(LANES)
    is_g = (lane >= ROUTER_GROUP_LANE0) & (lane < ROUTER_GROUP_LANE0 + N_GROUPS)
    gl = jnp.where(is_g, logits, NEG_INF)
    gmax = jnp.max(gl, axis=-1, keepdims=True)
    gsum = jnp.sum(jnp.exp(gl - gmax), axis=-1, keepdims=True)
    g_val = 1.0 / gsum
    g_idx = jnp.min(jnp.where(gl == gmax, lane, big), axis=-1, keepdims=True) - ROUTER_GROUP_LANE0
    e_lo = g_idx * EXPERTS_PER_GROUP
    in_g = (lane >= e_lo) & (lane < e_lo + EXPERTS_PER_GROUP)
    el = jnp.where(in_g, logits, NEG_INF)
    emax = jnp.max(el, axis=-1, keepdims=True)
    ee = jnp.exp(el - emax)
    e_prob = ee / jnp.sum(ee, axis=-1, keepdims=True)
    p1 = jnp.max(e_prob, axis=-1, keepdims=True)
    i1 = jnp.min(jnp.where(in_g & (e_prob == p1), lane, big), axis=-1, keepdims=True)
    rest = jnp.where(in_g & (lane != i1), e_prob, -1.0)
    p2 = jnp.max(rest, axis=-1, keepdims=True)
    i2 = jnp.min(jnp.where(rest == p2, lane, big), axis=-1, keepdims=True)
    denom = p1 + p2
    comb_ref[...] = (jnp.where(lane == i1, p1 / denom * g_val, 0.0)
                     + jnp.where(lane == i2, p2 / denom * g_val, 0.0))


def _mixer(x2, odn2, oda2, p2, ws, tm):
    n = x2.shape[0]
    row = lambda i: (i, 0)
    const = lambda i: (0, 0)

    def cspec(a):
        return pl.BlockSpec(a.shape, const)

    (eg, eb, wz, bz, wg, bg, wdn, wda, wout, g1, b1, wr_hi, wr_lo, br, wpg, bpg, wpp) = ws
    return pl.pallas_call(
        _mixer_kernel,
        grid=(n // tm,),
        in_specs=[
            pl.BlockSpec((tm, D_MODEL), row),
            pl.BlockSpec((tm, DN_WIDTH), row),
            pl.BlockSpec((tm, DA_WIDTH), row),
            pl.BlockSpec((tm, PLE_DIM), row),
        ] + [cspec(a) for a in ws],
        out_specs=[
            pl.BlockSpec((tm, D_MODEL), row),
            pl.BlockSpec((tm, D_MODEL), row),
            pl.BlockSpec((tm, LANES), row),
        ],
        out_shape=[
            jax.ShapeDtypeStruct((n, D_MODEL), BF16),
            jax.ShapeDtypeStruct((n, D_MODEL), F32),
            jax.ShapeDtypeStruct((n, LANES), F32),
        ],
        compiler_params=_cparams(("parallel",)),
        name="mixer",
    )(x2, odn2, oda2, p2, *ws)


def _moe_kernel(h_ref, base_ref, comb_ref, wgu_ref, wd_ref, g2_ref, b2_ref, o_ref, acc_ref):
    e = pl.program_id(1)

    @pl.when(e == 0)
    def _():
        acc_ref[...] = base_ref[...]

    comb = comb_ref[...]
    lane = lax.broadcasted_iota(jnp.int32, comb.shape, 1)
    c = jnp.sum(jnp.where(lane == e, comb, 0.0), axis=-1, keepdims=True)
    gu = _dot(h_ref[...], wgu_ref[...])
    hid = _silu(gu[:, :EXPERT_FF]) * gu[:, EXPERT_FF:]
    acc_ref[...] += _dot((hid * c).astype(BF16), wd_ref[...])

    @pl.when(e == N_EXPERTS - 1)
    def _():
        o_ref[...] = _layer_norm(acc_ref[...], g2_ref[...], b2_ref[...])


def _moe(h1b, base, comb, wgu, wd, g2, b2, tm):
    n = h1b.shape[0]
    row = lambda i, e: (i, 0)
    return pl.pallas_call(
        _moe_kernel,
        grid=(n // tm, N_EXPERTS),
        in_specs=[
            pl.BlockSpec((tm, D_MODEL), row),
            pl.BlockSpec((tm, D_MODEL), row),
            pl.BlockSpec((tm, LANES), row),
            pl.BlockSpec((None, D_MODEL, 2 * EXPERT_FF), lambda i, e: (e, 0, 0)),
            pl.BlockSpec((None, EXPERT_FF, D_MODEL), lambda i, e: (e, 0, 0)),
            pl.BlockSpec((1, D_MODEL), lambda i, e: (0, 0)),
            pl.BlockSpec((1, D_MODEL), lambda i, e: (0, 0)),
        ],
        out_specs=pl.BlockSpec((tm, D_MODEL), row),
        out_shape=jax.ShapeDtypeStruct((n, D_MODEL), F32),
        scratch_shapes=[pltpu.VMEM((tm, D_MODEL), F32)],
        compiler_params=_cparams(("parallel", "arbitrary")),
        name="moe",
    )(h1b, base, comb, wgu, wd, g2, b2)


def _row(v):
    return v.reshape(1, -1).astype(F32)


def _pad_lanes(v, offset=0):
    out = jnp.zeros((1, LANES), F32)
    return lax.dynamic_update_slice(out, v.reshape(1, -1).astype(F32), (0, offset))


def _tile(n, pref):
    t = min(pref, n)
    while n % t:
        t //= 2
    return t


def kernel(x, p, emb_ln_g, emb_ln_b, w_in, b_in, conv_w, dn_a_log, dn_dt_bias, dn_norm_w, w_dn_o,
           da_lq1, da_lk1, da_lq2, da_lk2, da_subln_w, w_da_o, w_out, ln1_g, ln1_b,
           w_router_group, b_router_group, w_router_expert, b_router_expert,
           w_exp_gate, w_exp_up, w_exp_down, w_ple_gate, b_ple_gate, w_ple_proj, ln2_g, ln2_b):
    B, S, _ = x.shape
    n = B * S
    assert S % CHUNK == 0
    li = 0
    offs = np.concatenate([[0], np.cumsum(IN_SIZES)]).astype(int)
    c_dn, c_z, c_b, c_a, c_q, c_k, c_v, c_g = [slice(int(offs[j]), int(offs[j + 1])) for j in range(8)]
    w_i, b_i = w_in[li], b_in[li]

    small_w = jnp.zeros((D_MODEL, LANES), F32)
    small_w = small_w.at[:, 0:DN_HEADS].set(w_i[:, c_b]).at[:, DN_HEADS:2 * DN_HEADS].set(w_i[:, c_a])
    small_b = jnp.zeros((LANES,), F32)
    small_b = small_b.at[0:DN_HEADS].set(b_i[c_b]).at[DN_HEADS:2 * DN_HEADS].set(b_i[c_a])
    w_a = jnp.concatenate([w_i[:, c_dn], small_w, w_i[:, c_q], w_i[:, c_k], w_i[:, c_v]], axis=1).astype(BF16)
    b_a = jnp.concatenate([b_i[c_dn], small_b, b_i[c_q], b_i[c_k], b_i[c_v]]).reshape(1, -1).astype(F32)
    alog = _pad_lanes(dn_a_log[li], DN_HEADS)
    dtb = _pad_lanes(dn_dt_bias[li], DN_HEADS)

    x2 = x.reshape(n, D_MODEL)
    tm_a = _tile(n, 512)
    dn_raw, ba, da = _in_proj(x2, _row(emb_ln_g), _row(emb_ln_b), w_a, b_a, alog, dtb, tm_a)

    tb = _tile(S, 256)
    ba3 = ba.reshape(B, S, LANES)
    gt4 = jnp.transpose(ba3[:, :, DN_HEADS:2 * DN_HEADS], (0, 2, 1)).reshape(B, DN_HEADS, 1, S)
    o_dn = _deltanet(dn_raw.reshape(B, S, DN_COLS), ba3, gt4, conv_w[li].astype(F32),
                     _row(dn_norm_w[li]), tb)

    slopes = jnp.asarray(2.0 ** (-8.0 * np.arange(1, DA_HEADS + 1) / DA_HEADS), dtype=F32)
    slopes = jnp.broadcast_to(slopes[:, None, None], (DA_HEADS, 1, LANES))
    tq = _tile(S, 512)
    o_da = _diffattn(da.reshape(B, S, DA_COLS), _row(da_lq1[li]), _row(da_lk1[li]), _row(da_lq2[li]),
                     _row(da_lk2[li]), slopes, _row(da_subln_w[li]), tq)

    wr = jnp.zeros((D_MODEL, LANES), F32)
    wr = wr.at[:, 0:N_EXPERTS].set(w_router_expert[li])
    wr = wr.at[:, ROUTER_GROUP_LANE0:ROUTER_GROUP_LANE0 + N_GROUPS].set(w_router_group[li])
    wr_hi = wr.astype(BF16)
    wr_lo = (wr - wr_hi.astype(F32)).astype(BF16)
    br = jnp.zeros((LANES,), F32).at[0:N_EXPERTS].set(b_router_expert[li])
    br = br.at[ROUTER_GROUP_LANE0:ROUTER_GROUP_LANE0 + N_GROUPS].set(b_router_group[li]).reshape(1, LANES)
    ws = (_row(emb_ln_g), _row(emb_ln_b),
          w_i[:, c_z].astype(BF16), _row(b_i[c_z]),
          w_i[:, c_g].astype(BF16), _row(b_i[c_g]),
          w_dn_o[li].astype(BF16), w_da_o[li].astype(BF16), w_out[li].astype(BF16),
          _row(ln1_g[li]), _row(ln1_b[li]), wr_hi, wr_lo, br,
          w_ple_gate[li].astype(BF16), _row(b_ple_gate[li]), w_ple_proj[li].astype(BF16))
    tm_c = _tile(n, 256)
    h1b, base, comb = _mixer(x2, o_dn.reshape(n, DN_WIDTH), o_da.reshape(n, DA_WIDTH),
                             p[li].reshape(n, PLE_DIM), ws, tm_c)

    wgu = jnp.concatenate([w_exp_gate[li], w_exp_up[li]], axis=-1).astype(BF16)
    wd = w_exp_down[li].astype(BF16)
    tm_d = _tile(n, 1024)
    out = _moe(h1b, base, comb, wgu, wd, _row(ln2_g[li]), _row(ln2_b[li]), tm_d)
    return out.reshape(B, S, D_MODEL)
```

```python
import functools
import math

import jax
import jax.numpy as jnp
import numpy as np
from jax import lax
from jax.experimental import pallas as pl
from jax.experimental.pallas import tpu as pltpu

F32 = jnp.float32
BF16 = jnp.bfloat16

D_MODEL = 1024
PLE_DIM = 256
DN_HEADS = 4
DN_HEAD_DIM = 128
DN_WIDTH = DN_HEADS * DN_HEAD_DIM
CONV_WIDTH = 4
CHUNK = 64
DA_HEADS = 4
DA_HEAD_DIM = 64
DA_V_DIM = 2 * DA_HEAD_DIM
DA_QK_WIDTH = DA_HEADS * 2 * DA_HEAD_DIM
DA_WIDTH = DA_HEADS * DA_V_DIM
N_GROUPS = 4
EXPERTS_PER_GROUP = 4
N_EXPERTS = N_GROUPS * EXPERTS_PER_GROUP
EXPERT_FF = 256
IN_SIZES = (3 * DN_WIDTH, DN_WIDTH, DN_HEADS, DN_HEADS, DA_QK_WIDTH, DA_QK_WIDTH, DA_WIDTH, 2 * D_MODEL)
LN_EPS = 1e-5
RMS_EPS = 1e-6
DEPTH = 1
ALPHA = (2.0 * DEPTH) ** 0.25
LAM_INIT = 0.8 - 0.6 * math.exp(-0.3 * 0)

LANES = 128
VMEM_LIMIT = 56 * 1024 * 1024

NEG_INF = float("-inf")


def _cparams(sem, flags=None):
    return pltpu.CompilerParams(dimension_semantics=sem, vmem_limit_bytes=VMEM_LIMIT, flags=flags)


def _layer_norm(x, g, b):
    mu = jnp.mean(x, axis=-1, keepdims=True)
    xc = x - mu
    var = jnp.mean(xc * xc, axis=-1, keepdims=True)
    return xc * lax.rsqrt(var + LN_EPS) * g + b


def _sigmoid(x):
    return 1.0 / (1.0 + jnp.exp(-x))


def _silu(x):
    return x * _sigmoid(x)


def _dot(a, b):
    return jnp.dot(a, b, preferred_element_type=F32)


def _dot_nt(a, b):
    return lax.dot_general(a, b, (((1,), (1,)), ((), ())), preferred_element_type=F32)


DN_COLS = 3 * DN_WIDTH
DA_V_PAD = 16
DA_V_AUG = DA_V_DIM + DA_V_PAD
DA_TK = 512
DA_TQ = DA_TK
LOG2E = 1.4426950408889634


IN_PROJ_ROW_PARTS = 2
TAIL = 8


def _in_proj_kernel(tiles_per_seq, x_ref, g_ref, b_ref, w_ref, bias_ref, alog_ref, dtb_ref, cw_ref,
                    dn_ref, ba_ref, qk_ref, vt_ref, xe_ref):
    tm = x_ref.shape[0]
    assert vt_ref.shape[0] == 1 and tm == DA_TK
    n_parts = IN_PROJ_ROW_PARTS
    pm = tm // n_parts
    parts = [slice(r * pm, (r + 1) * pm) for r in range(n_parts)]
    c0 = DN_COLS + LANES
    qw = DA_QK_WIDTH

    def proj(hb, lo, hi):
        return _dot(hb, w_ref[:, lo:hi]) + bias_ref[:, lo:hi]

    @pl.when(pl.program_id(0) % tiles_per_seq == 0)
    def _():
        xe_ref[tm:tm + TAIL, :] = jnp.zeros((TAIL, DN_COLS), F32)

    xe_ref[0:TAIL, :] = xe_ref[tm:tm + TAIL, :]
    hbs = [_layer_norm(x_ref[rows, :], g_ref[...], b_ref[...]).astype(BF16) for rows in parts]
    for hb, rows in zip(hbs, parts):
        xe_ref[TAIL + rows.start:TAIL + rows.stop, :] = proj(hb, 0, DN_COLS)
    for hb, rows in zip(hbs, parts):
        qk_ref[rows, :qw] = (proj(hb, c0, c0 + qw) * (DA_HEAD_DIM ** -0.5 * LOG2E)).astype(BF16)
        qk_ref[rows, qw:] = proj(hb, c0 + qw, c0 + 2 * qw).astype(BF16)
    for hb, rows in zip(hbs, parts):
        v_t = proj(hb, c0 + 2 * qw, c0 + 3 * qw).T.astype(BF16)
        extra = (lax.broadcasted_iota(jnp.int32, (DA_V_PAD, pm), 0) == 0).astype(F32).astype(BF16)
        vt_ref[0, :, rows] = jnp.concatenate(
            [blk for hd in range(DA_HEADS) for blk in (v_t[hd * DA_V_DIM:(hd + 1) * DA_V_DIM], extra)], axis=0)
    for hb, rows in zip(hbs, parts):
        ba = proj(hb, DN_COLS, DN_COLS + LANES)
        lane = lax.broadcasted_iota(jnp.int32, (pm, LANES), 1)
        row = lax.broadcasted_iota(jnp.int32, (pm, LANES), 0)
        beta = _sigmoid(ba)
        xs = ba + dtb_ref[...]
        softplus = jnp.maximum(xs, 0.0) + jnp.log(1.0 + jnp.exp(-jnp.abs(xs)))
        g = -jnp.exp(alog_ref[...]) * softplus
        g = jnp.where((lane >= DN_HEADS) & (lane < 2 * DN_HEADS), g, 0.0)
        rin = row & (CHUNK - 1)
        k = 1
        while k < CHUNK:
            g = g + jnp.where(rin >= k, pltpu.roll(g, k, 0), 0.0)
            k *= 2
        ba_ref[rows, :] = jnp.where(lane < DN_HEADS, beta, g)

    cw = cw_ref[...]
    hd = DN_HEAD_DIM
    for rows in parts:
        for hh in range(3 * DN_HEADS):
            cols = slice(hh * hd, (hh + 1) * hd)
            r0 = TAIL + rows.start
            xh = xe_ref[r0:r0 + pm, cols] * cw[CONV_WIDTH - 1:CONV_WIDTH, cols]
            for j in range(CONV_WIDTH - 1):
                off = r0 - (CONV_WIDTH - 1) + j
                xh = xh + xe_ref[off:off + pm, cols] * cw[j:j + 1, cols]
            xh = _silu(xh)
            if hh < 2 * DN_HEADS:
                scale = (hd ** -0.5) if hh < DN_HEADS else 1.0
                xh = xh * lax.rsqrt(jnp.sum(xh * xh, axis=-1, keepdims=True) + RMS_EPS) * scale
            dn_ref[rows, cols] = xh


def _in_proj(x2, emb_g, emb_b, w_a, b_a, alog, dtb, conv_w, tm, seq):
    n = x2.shape[0]
    wcols = w_a.shape[1]
    assert seq % tm == 0
    const = lambda i: (0, 0)
    return pl.pallas_call(
        functools.partial(_in_proj_kernel, seq // tm),
        grid=(n // tm,),
        in_specs=[
            pl.BlockSpec((tm, D_MODEL), lambda i: (i, 0)),
            pl.BlockSpec((1, D_MODEL), const),
            pl.BlockSpec((1, D_MODEL), const),
            pl.BlockSpec((D_MODEL, wcols), const),
            pl.BlockSpec((1, wcols), const),
            pl.BlockSpec((1, LANES), const),
            pl.BlockSpec((1, LANES), const),
            pl.BlockSpec((CONV_WIDTH, DN_COLS), const),
        ],
        out_specs=[
            pl.BlockSpec((tm, DN_COLS), lambda i: (i, 0)),
            pl.BlockSpec((tm, LANES), lambda i: (i, 0)),
            pl.BlockSpec((tm, 2 * DA_QK_WIDTH), lambda i: (i, 0)),
            pl.BlockSpec((tm // DA_TK, DA_HEADS * DA_V_AUG, DA_TK), lambda i: (i, 0, 0)),
        ],
        out_shape=[
            jax.ShapeDtypeStruct((n, DN_COLS), F32),
            jax.ShapeDtypeStruct((n, LANES), F32),
            jax.ShapeDtypeStruct((n, 2 * DA_QK_WIDTH), BF16),
            jax.ShapeDtypeStruct((n // DA_TK, DA_HEADS * DA_V_AUG, DA_TK), BF16),
        ],
        scratch_shapes=[pltpu.VMEM((tm + 2 * TAIL, DN_COLS), F32)],
        compiler_params=_cparams(("arbitrary",)),
        name="in_proj",
    )(x2, emb_g, emb_b, w_a, b_a, alog, dtb, conv_w)


DN_GROUP_CHUNKS = 4


def _dn_kernel(x_ref, ba_ref, gt_ref, nw_ref, o_ref, state_ref):
    sblk = pl.program_id(1)
    tb = x_ref.shape[0]
    nck = tb // CHUNK
    hh = DN_HEADS
    hd = DN_HEAD_DIM

    @pl.when(sblk == 0)
    def _():
        state_ref[...] = jnp.zeros_like(state_ref)

    ba = ba_ref[...]
    ri = lax.broadcasted_iota(jnp.int32, (CHUNK, CHUNK), 0)
    ci = lax.broadcasted_iota(jnp.int32, (CHUNK, CHUNK), 1)
    causal = ri >= ci
    strict = ri > ci
    eye = jnp.where(ri == ci, 1.0, 0.0).astype(F32)
    nw = nw_ref[...]

    q_b, k_b, kb_b, rhs_uw, qg_b, kgt_b, decay, eg_last = {}, {}, {}, {}, {}, {}, {}, {}
    for h in range(hh):
        q = x_ref[:, h * hd:(h + 1) * hd]
        k = x_ref[:, (hh + h) * hd:(hh + h + 1) * hd]
        v = x_ref[:, (2 * hh + h) * hd:(2 * hh + h + 1) * hd]
        bcol = ba[:, h:h + 1]
        gcol = ba[:, hh + h:hh + h + 1]
        grow = gt_ref[h]
        egcol = jnp.exp(gcol)
        kb = k * bcol
        uw = jnp.concatenate([v * bcol, kb * egcol], axis=1).astype(BF16)
        qg = (q * egcol).astype(BF16)
        qb = q.astype(BF16)
        kbb = kb.astype(BF16)
        kk = k.astype(BF16)
        for c in range(nck):
            sl = slice(c * CHUNK, (c + 1) * CHUNK)
            gc = gcol[sl]
            g_last = gc[CHUNK - 1:CHUNK, :]
            q_b[h, c], k_b[h, c], kb_b[h, c] = qb[sl], kk[sl], kbb[sl]
            rhs_uw[h, c], qg_b[h, c] = uw[sl], qg[sl]
            kgt_b[h, c] = (k[sl] * jnp.exp(g_last - gc)).T.astype(BF16)
            decay[h, c] = jnp.exp(jnp.where(causal, gc - grow[:, sl], NEG_INF))
            eg_last[h, c] = jnp.exp(g_last)

    power, t, a_intra, u, w_b = {}, {}, {}, {}, {}
    n_rounds = int(math.log2(CHUNK)) - 1

    def local_stages(group):
        def s_nmat():
            for p_ in group:
                nmat = jnp.where(strict, -_dot_nt(kb_b[p_], k_b[p_]) * decay[p_], 0.0)
                power[p_] = nmat
                t[p_] = eye + nmat

        def s_intra():
            for p_ in group:
                a_intra[p_] = (_dot_nt(q_b[p_], k_b[p_]) * decay[p_]).astype(BF16)

        def s_square():
            for p_ in group:
                pb = power[p_].astype(BF16)
                power[p_] = _dot(pb, pb)

        def s_round(last):
            for p_ in group:
                pb = power[p_].astype(BF16)
                if last:
                    t[p_] = t[p_] + _dot(t[p_].astype(BF16), pb)
                else:
                    both = _dot(jnp.concatenate([power[p_], t[p_]], axis=0).astype(BF16), pb)
                    power[p_] = both[:CHUNK]
                    t[p_] = t[p_] + both[CHUNK:]

        def s_uw():
            for p_ in group:
                uw = _dot(t[p_].astype(BF16), rhs_uw[p_])
                u[p_] = uw[:, :hd]
                w_b[p_] = uw[:, hd:].astype(BF16)

        rounds = [functools.partial(s_round, r == n_rounds - 1) for r in range(n_rounds)]
        return [s_nmat, s_intra, s_square] + rounds + [s_uw]

    states = [state_ref[h] for h in range(hh)]
    ws = {}

    def recur_stages(chunks):
        def r_first(c):
            for h in range(hh):
                lhs = jnp.concatenate([w_b[h, c], qg_b[h, c]], axis=0)
                ws[h] = _dot(lhs, states[h].astype(BF16))

        def r_second(c):
            for h in range(hh):
                v_new = (u[h, c] - ws[h][:CHUNK]).astype(BF16)
                lhs = jnp.concatenate([a_intra[h, c], kgt_b[h, c]], axis=0)
                both = _dot(lhs, v_new)
                o_c = ws[h][CHUNK:] + both[:CHUNK]
                states[h] = states[h] * eg_last[h, c] + both[CHUNK:]
                o_n = o_c * lax.rsqrt(jnp.mean(o_c * o_c, axis=-1, keepdims=True) + RMS_EPS) * nw
                o_ref[c * CHUNK:(c + 1) * CHUNK, h * hd:(h + 1) * hd] = o_n

        return [functools.partial(f, c) for c in chunks for f in (r_first, r_second)]

    gsz = min(DN_GROUP_CHUNKS, nck)
    groups = [list(range(g0, g0 + gsz)) for g0 in range(0, nck, gsz)]
    pending = []
    for chunks in groups:
        local = local_stages([(h, c) for c in chunks for h in range(hh)])
        for k, stage in enumerate(local):
            stage()
            if pending:
                pending.pop(0)()
        while pending:
            pending.pop(0)()
        pending = recur_stages(chunks)
    while pending:
        pending.pop(0)()
    for h in range(hh):
        state_ref[h] = states[h]


def _deltanet(dn3, ba3, gt4, norm_w, tb):
    b, s, _ = dn3.shape
    hh = DN_HEADS
    return pl.pallas_call(
        _dn_kernel,
        grid=(b, s // tb),
        in_specs=[
            pl.BlockSpec((None, tb, DN_COLS), lambda bi, i: (bi, i, 0)),
            pl.BlockSpec((None, tb, LANES), lambda bi, i: (bi, i, 0)),
            pl.BlockSpec((None, hh, 1, tb), lambda bi, i: (bi, 0, 0, i)),
            pl.BlockSpec((1, LANES), lambda bi, i: (0, 0)),
        ],
        out_specs=pl.BlockSpec((None, tb, DN_WIDTH), lambda bi, i: (bi, i, 0)),
        out_shape=jax.ShapeDtypeStruct((b, s, DN_WIDTH), F32),
        scratch_shapes=[pltpu.VMEM((hh, DN_HEAD_DIM, DN_HEAD_DIM), F32)],
        compiler_params=_cparams(("parallel", "arbitrary")),
        name="deltanet",
    )(dn3, ba3, gt4, norm_w)


N_BIAS_LANES = 3
DA_UNROLL = 2


def _da_kernel(lq1_ref, lk1_ref, lq2_ref, lk2_ref, slope_ref, q_ref, k_ref, vt_ref, w_ref, o_ref,
                ka1_ref, ka2_ref, *scr):
    i = pl.program_id(2)
    tq = q_ref.shape[0]
    s_refs = ((scr[0], scr[1]), (scr[2], scr[3]))
    p_refs = ((scr[4], scr[5]), (scr[6], scr[7]))
    acc_refs = (scr[8], scr[9])
    st_ref = scr[10]
    kn_ref = scr[11]
    nkt, tk, _ = ka1_ref.shape
    half = DA_HEAD_DIM

    @pl.when(i == 0)
    def _():
        c = slope_ref[...] * LOG2E
        lane = lax.broadcasted_iota(jnp.int32, (tk, LANES), 1)
        row = lax.broadcasted_iota(jnp.int32, (tk, LANES), 0)

        def build(t, carry):
            kb = k_ref[pl.ds(pl.multiple_of(t * tk, tk), tk), :].astype(F32)
            bias = c * (t * tk + row).astype(F32)
            hi = bias.astype(BF16).astype(F32)
            rem = bias - hi
            mid = rem.astype(BF16).astype(F32)
            lo = rem - mid

            def feats(l0):
                return jnp.where(lane == l0, hi, jnp.where(lane == l0 + 1, mid,
                                                           jnp.where(lane == l0 + 2, lo, 0.0)))

            ka1_ref[t] = jnp.where(lane < half, kb, feats(half)).astype(BF16)
            ka2_ref[t] = jnp.where(lane >= half, kb, feats(0)).astype(BF16)
            sq = kb * kb
            n1 = jnp.max(jnp.sum(jnp.where(lane < half, sq, 0.0), axis=-1, keepdims=True), axis=0, keepdims=True)
            n2 = jnp.max(jnp.sum(jnp.where(lane >= half, sq, 0.0), axis=-1, keepdims=True), axis=0, keepdims=True)
            kn1, kn2 = carry
            return jnp.where(tile_lane == t, n1, kn1), jnp.where(tile_lane == t, n2, kn2)

        tile_lane = lax.broadcasted_iota(jnp.int32, (1, LANES), 1)
        zeros = jnp.zeros((1, LANES), F32)
        kn1, kn2 = lax.fori_loop(0, nkt, build, (zeros, zeros))
        kn_ref[0:1, :] = kn1
        kn_ref[1:2, :] = kn2

    qv = q_ref[...].astype(F32)
    lane_q = lax.broadcasted_iota(jnp.int32, qv.shape, 1)
    q1 = jnp.where(lane_q < half, qv, jnp.where(lane_q < half + N_BIAS_LANES, 1.0, 0.0))
    q2 = jnp.where(lane_q >= half, qv, jnp.where(lane_q < N_BIAS_LANES, 1.0, 0.0))
    qts = (q1.T.astype(BF16), q2.T.astype(BF16))
    kas = (ka1_ref, ka2_ref)

    SKIP_BELOW = -170.0
    NORM_SLACK = 1.01
    qsq = qv * qv
    qn = [jnp.sqrt(jnp.max(jnp.sum(jnp.where(msk, qsq, 0.0), axis=-1, keepdims=True), axis=0, keepdims=True))
          for msk in (lane_q < half, lane_q >= half)]
    tile_lane = lax.broadcasted_iota(jnp.int32, (1, LANES), 1)
    gap = ((i - tile_lane - 1) * tk + 1).astype(F32)
    c_row = slope_ref[...] * LOG2E
    bound = None
    for mp in range(2):
        kn = jnp.sqrt(kn_ref[mp:mp + 1, :])
        kd = jnp.max(jnp.where(tile_lane == i, kn, 0.0), axis=-1, keepdims=True)
        b_mp = qn[mp] * (kn + kd) * NORM_SLACK - c_row * gap
        bound = b_mp if bound is None else jnp.maximum(bound, b_mp)
    skippable = (bound < SKIP_BELOW) & (tile_lane < i)
    first_keep = jnp.min(jnp.where(skippable, float(LANES), tile_lane.astype(F32)), axis=-1, keepdims=True)
    base = first_keep.astype(jnp.int32)[0, 0]
    cnt = i - base

    M_ROW, A_ROW = 0, 2
    CH = 128
    SUB = 8
    V_AFTER_CHUNK = DA_TK // CH - 1

    def row(mp, r):
        return st_ref[mp, r:r + 1, :]

    def fold(x, op):
        return op(x.reshape(CH // SUB, SUB, tq), axis=0)

    def step(t, slot, s_mode, do_v=True, do_e=True):
        o = 1 - slot
        a_prevs = [row(mp, A_ROW + o) for mp in range(2)]
        for mp in range(2):
            m_t = row(mp, M_ROW + slot)
            tmax = jnp.full((SUB, tq), NEG_INF, F32)
            for c in range(tk // CH):
                rows = slice(c * CH, (c + 1) * CH)
                if do_e:
                    pc = jnp.exp2(s_refs[slot][mp][rows, :] - m_t)
                    p_refs[slot][mp][rows, :] = pc.astype(BF16)
                if s_mode is not None:
                    sc = _dot(kas[mp][base + t + 1, rows, :], qts[mp])
                    if s_mode != "plain":
                        krow = lax.broadcasted_iota(jnp.int32, (CH, tq), 0) + (c * CH + s_mode)
                        qcol = lax.broadcasted_iota(jnp.int32, (CH, tq), 1)
                        sc = jnp.where(krow <= qcol, sc, NEG_INF)
                    s_refs[o][mp][rows, :] = sc
                    tmax = jnp.maximum(tmax, fold(sc, jnp.max))
                if do_v and c == V_AFTER_CHUNK:
                    vt = vt_ref[base + jnp.maximum(t - 1, 0)]
                    acc_refs[mp][...] = a_prevs[mp] * acc_refs[mp][...] + _dot(vt, p_refs[o][mp][...])
            if s_mode is not None:
                m_new = jnp.maximum(m_t, jnp.max(tmax, axis=0, keepdims=True))
                st_ref[mp, M_ROW + o:M_ROW + o + 1, :] = m_new
                st_ref[mp, A_ROW + o:A_ROW + o + 1, :] = jnp.exp2(m_t - m_new)

    for mp in range(2):
        p_refs[1][mp][...] = jnp.zeros((tk, tq), BF16)
        acc_refs[mp][...] = jnp.zeros((DA_V_AUG, tq), F32)
        st_ref[mp, M_ROW + 1:M_ROW + 2, :] = jnp.full((1, tq), -1e30, F32)
        st_ref[mp, A_ROW + 1:A_ROW + 2, :] = jnp.ones((1, tq), F32)

    @pl.when(cnt == 0)
    def _():
        step(-1, 1, 0, do_v=False, do_e=False)
        step(0, 0, None)
        step(1, 1, None, do_e=False)

    @pl.when(cnt > 0)
    def _():
        step(-1, 1, "plain", do_v=False, do_e=False)

    unroll = DA_UNROLL

    def body(it, carry):
        for k in range(unroll):
            step(unroll * it + k, k & 1, "plain")
        return carry

    n_plain = jnp.maximum(cnt - 1, 0)
    lax.fori_loop(0, n_plain // unroll, body, 0)

    for rem in range(unroll):
        @pl.when((cnt > 0) & (n_plain % unroll == rem))
        def _(rem=rem):
            for k in range(rem):
                step(cnt - 1 - rem + k, k & 1, "plain")
            step(cnt - 1, rem & 1, 0)
            step(cnt, (rem + 1) & 1, None)
            step(cnt + 1, rem & 1, None, do_e=False)

    l1 = acc_refs[0][DA_V_DIM:DA_V_DIM + 1, :]
    l2 = acc_refs[1][DA_V_DIM:DA_V_DIM + 1, :]
    a1 = acc_refs[0][0:DA_V_DIM, :]
    a2 = acc_refs[1][0:DA_V_DIM, :]

    lam = (jnp.exp(jnp.sum(lq1_ref[...] * lk1_ref[...], axis=-1, keepdims=True))
           - jnp.exp(jnp.sum(lq2_ref[...] * lk2_ref[...], axis=-1, keepdims=True)) + LAM_INIT)
    o_t = a1 / l1 - lam * (a2 / l2)
    o_t = o_t * lax.rsqrt(jnp.mean(o_t * o_t, axis=0, keepdims=True) + RMS_EPS) * (1.0 - LAM_INIT)
    o_ref[...] = (o_t.T * w_ref[...]).astype(BF16)


def _diffattn(qk3, vt3, lq1, lk1, lq2, lk2, slopes, subln_w):
    b, s, _ = qk3.shape
    hh = DA_HEADS
    tq, tk = DA_TQ, DA_TK
    assert tq == tk and s % tq == 0
    nkt = s // tk
    c2 = lambda bi, h, i: (0, 0)
    return pl.pallas_call(
        _da_kernel,
        grid=(b, hh, s // tq),
        in_specs=[
            pl.BlockSpec((1, DA_HEAD_DIM), c2),
            pl.BlockSpec((1, DA_HEAD_DIM), c2),
            pl.BlockSpec((1, DA_HEAD_DIM), c2),
            pl.BlockSpec((1, DA_HEAD_DIM), c2),
            pl.BlockSpec((None, 1, LANES), lambda bi, h, i: (h, 0, 0)),
            pl.BlockSpec((None, tq, LANES), lambda bi, h, i: (bi, i, h)),
            pl.BlockSpec((None, s, LANES), lambda bi, h, i: (bi, 0, hh + h)),
            pl.BlockSpec((nkt, DA_V_AUG, tk), lambda bi, h, i: (bi, h, 0)),
            pl.BlockSpec((1, DA_V_DIM), c2),
        ],
        out_specs=pl.BlockSpec((None, tq, LANES), lambda bi, h, i: (bi, i, h)),
        out_shape=jax.ShapeDtypeStruct((b, s, DA_WIDTH), BF16),
        scratch_shapes=[
            pltpu.VMEM((nkt, tk, LANES), BF16),
            pltpu.VMEM((nkt, tk, LANES), BF16),
        ] + [pltpu.VMEM((tk, tq), F32)] * 4 + [pltpu.VMEM((tk, tq), BF16)] * 4 + [
            pltpu.VMEM((DA_V_AUG, tq), F32),
            pltpu.VMEM((DA_V_AUG, tq), F32),
            pltpu.VMEM((2, 8, tq), F32),
            pltpu.VMEM((8, LANES), F32),
        ],
        compiler_params=_cparams(("parallel", "parallel", "arbitrary")),
        name="diffattn",
    )(lq1, lk1, lq2, lk2, slopes, qk3, qk3, vt3, subln_w)


ROUTER_GROUP_LANE0 = N_EXPERTS
MIXER_ROW_PARTS = 2


def _mixer_kernel(x_ref, odn_ref, oda_ref, p_ref, eg_ref, eb_ref, wz_ref, bz_ref, wg_ref, bg_ref,
                  wdn_ref, wda_ref, wout_ref, g1_ref, b1_ref, wr_cat_ref, wr_hi_ref, br_ref,
                  wpg_ref, bpg_ref, wpp_ref, h1b_ref, base_ref, comb_ref):
    n_parts = MIXER_ROW_PARTS
    pm = x_ref.shape[0] // n_parts
    parts = [slice(r * pm, (r + 1) * pm) for r in range(n_parts)]
    st = [dict() for _ in parts]
    for s, rows in zip(st, parts):
        s["h"] = _layer_norm(x_ref[rows, :], eg_ref[...], eb_ref[...])
        s["hb"] = s["h"].astype(BF16)
    for s, rows in zip(st, parts):
        s["z"] = _dot(s["hb"], wz_ref[...]) + bz_ref[...]
        s["gl"] = _dot(s["hb"], wg_ref[...]) + bg_ref[...]
    for s, rows in zip(st, parts):
        s["y_dn"] = _dot((odn_ref[rows, :] * _silu(s["z"])).astype(BF16), wdn_ref[...])
        s["y_da"] = _dot(oda_ref[rows, :], wda_ref[...])
    for s, rows in zip(st, parts):
        gates = _sigmoid(s["gl"])
        mixed = gates[:, :D_MODEL] * s["y_dn"] + gates[:, D_MODEL:] * s["y_da"]
        s["mix"] = _dot(mixed.astype(BF16), wout_ref[...])
    for s, rows in zip(st, parts):
        s["h1"] = _layer_norm(ALPHA * s["h"] + s["mix"], g1_ref[...], b1_ref[...])
        s["h1b"] = s["h1"].astype(BF16)
        h1b_ref[rows, :] = s["h1b"]
    for s, rows in zip(st, parts):
        ple = (_sigmoid(_dot(s["h1b"], wpg_ref[...]) + bpg_ref[...])
               * _dot(p_ref[rows, :].astype(BF16), wpp_ref[...]))
        base_ref[rows, :] = ALPHA * s["h1"] + ple
        h1_lo = (s["h1"] - s["h1b"].astype(F32)).astype(BF16)
        both = _dot(s["h1b"], wr_cat_ref[...])
        s["logits"] = (both[:, :LANES] + both[:, LANES:] + _dot(h1_lo, wr_hi_ref[...]) + br_ref[...])
    for s, rows in zip(st, parts):
        comb_ref[rows, :] = _route(s["logits"])


def _route(logits):
    tm = logits.shape[0]
    lane = lax.broadcasted_iota(jnp.int32, (tm, LANES), 1).astype(F32)
    big = float(LANES)
    is_g = (lane >= ROUTER_GROUP_LANE0) & (lane < ROUTER_GROUP_LANE0 + N_GROUPS)
    gl = jnp.where(is_g, logits, NEG_INF)
    gmax = jnp.max(gl, axis=-1, keepdims=True)
    gsum = jnp.sum(jnp.exp(gl - gmax), axis=-1, keepdims=True)
    g_val = 1.0 / gsum
    g_idx = jnp.min(jnp.where(gl == gmax, lane, big), axis=-1, keepdims=True) - ROUTER_GROUP_LANE0
    e_lo = g_idx * EXPERTS_PER_GROUP
    in_g = (lane >= e_lo) & (lane < e_lo + EXPERTS_PER_GROUP)
    el = jnp.where(in_g, logits, NEG_INF)
    emax = jnp.max(el, axis=-1, keepdims=True)
    ee = jnp.exp(el - emax)
    e_prob = ee / jnp.sum(ee, axis=-1, keepdims=True)
    p1 = jnp.max(e_prob, axis=-1, keepdims=True)
    i1 = jnp.min(jnp.where(in_g & (e_prob == p1), lane, big), axis=-1, keepdims=True)
    rest = jnp.where(in_g & (lane != i1), e_prob, -1.0)
    p2 = jnp.max(rest, axis=-1, keepdims=True)
    i2 = jnp.min(jnp.where(rest == p2, lane, big), axis=-1, keepdims=True)
    denom = p1 + p2
    return (jnp.where(lane == i1, p1 / denom * g_val, 0.0)
            + jnp.where(lane == i2, p2 / denom * g_val, 0.0))


def _mixer(x2, odn2, oda2, p2, ws, tm):
    n = x2.shape[0]
    row = lambda i: (i, 0)
    const = lambda i: (0, 0)

    def cspec(a):
        return pl.BlockSpec(a.shape, const, pipeline_mode=pl.Buffered(1))

    (eg, eb, wz, bz, wg, bg, wdn, wda, wout, g1, b1, wr_cat, wr_hi, br, wpg, bpg, wpp) = ws
    return pl.pallas_call(
        _mixer_kernel,
        grid=(n // tm,),
        in_specs=[
            pl.BlockSpec((tm, D_MODEL), row),
            pl.BlockSpec((tm, DN_WIDTH), row),
            pl.BlockSpec((tm, DA_WIDTH), row),
            pl.BlockSpec((tm, PLE_DIM), row),
        ] + [cspec(a) for a in ws],
        out_specs=[
            pl.BlockSpec((tm, D_MODEL), row),
            pl.BlockSpec((tm, D_MODEL), row),
            pl.BlockSpec((tm, LANES), row),
        ],
        out_shape=[
            jax.ShapeDtypeStruct((n, D_MODEL), BF16),
            jax.ShapeDtypeStruct((n, D_MODEL), F32),
            jax.ShapeDtypeStruct((n, LANES), F32),
        ],
        compiler_params=_cparams(("parallel",)),
        name="mixer",
    )(x2, odn2, oda2, p2, *ws)


def _moe_kernel(h_ref, base_ref, comb_ref, wg_ref, wu_ref, wd_ref, g2_ref, b2_ref, o_ref, acc_ref):
    g = pl.program_id(1)

    @pl.when(g == 0)
    def _():
        acc_ref[...] = base_ref[...]

    comb = comb_ref[...]
    lane = lax.broadcasted_iota(jnp.int32, comb.shape, 1)
    h = h_ref[...]
    cols = []
    for e in range(EXPERTS_PER_GROUP):
        c = jnp.sum(jnp.where(lane == g * EXPERTS_PER_GROUP + e, comb, 0.0), axis=-1, keepdims=True)
        hid = _silu(_dot(h, wg_ref[e])) * _dot(h, wu_ref[e])
        cols.append((hid * c).astype(BF16))
    hid_all = jnp.concatenate(cols, axis=1)
    wd = wd_ref[...].reshape(EXPERTS_PER_GROUP * EXPERT_FF, D_MODEL)
    acc_ref[...] += _dot(hid_all, wd)

    @pl.when(g == N_GROUPS - 1)
    def _():
        o_ref[...] = _layer_norm(acc_ref[...], g2_ref[...], b2_ref[...])


def _moe(h1b, base, comb, wg, wu, wd, g2, b2, tm):
    n = h1b.shape[0]
    row = lambda i, g: (i, 0)
    return pl.pallas_call(
        _moe_kernel,
        grid=(n // tm, N_GROUPS),
        in_specs=[
            pl.BlockSpec((tm, D_MODEL), row),
            pl.BlockSpec((tm, D_MODEL), row),
            pl.BlockSpec((tm, LANES), row),
            pl.BlockSpec((EXPERTS_PER_GROUP, D_MODEL, EXPERT_FF), lambda i, g: (g, 0, 0)),
            pl.BlockSpec((EXPERTS_PER_GROUP, D_MODEL, EXPERT_FF), lambda i, g: (g, 0, 0)),
            pl.BlockSpec((EXPERTS_PER_GROUP, EXPERT_FF, D_MODEL), lambda i, g: (g, 0, 0)),
            pl.BlockSpec((1, D_MODEL), lambda i, g: (0, 0)),
            pl.BlockSpec((1, D_MODEL), lambda i, g: (0, 0)),
        ],
        out_specs=pl.BlockSpec((tm, D_MODEL), row),
        out_shape=jax.ShapeDtypeStruct((n, D_MODEL), F32),
        scratch_shapes=[pltpu.VMEM((tm, D_MODEL), F32)],
        compiler_params=_cparams(("parallel", "arbitrary")),
        name="moe",
    )(h1b, base, comb, wg, wu, wd, g2, b2)


def _row(v):
    return v.reshape(1, -1).astype(F32)


def _pad_lanes(v, offset=0):
    out = jnp.zeros((1, LANES), F32)
    return lax.dynamic_update_slice(out, v.reshape(1, -1).astype(F32), (0, offset))


def _tile(n, pref):
    t = min(pref, n)
    while n % t:
        t //= 2
    return t


def kernel(x, p, emb_ln_g, emb_ln_b, w_in, b_in, conv_w, dn_a_log, dn_dt_bias, dn_norm_w, w_dn_o,
           da_lq1, da_lk1, da_lq2, da_lk2, da_subln_w, w_da_o, w_out, ln1_g, ln1_b,
           w_router_group, b_router_group, w_router_expert, b_router_expert,
           w_exp_gate, w_exp_up, w_exp_down, w_ple_gate, b_ple_gate, w_ple_proj, ln2_g, ln2_b):
    B, S, _ = x.shape
    n = B * S
    assert S % CHUNK == 0
    li = 0
    offs = np.concatenate([[0], np.cumsum(IN_SIZES)]).astype(int)
    c_dn, c_z, c_b, c_a, c_q, c_k, c_v, c_g = [slice(int(offs[j]), int(offs[j + 1])) for j in range(8)]
    w_i, b_i = w_in[li], b_in[li]

    small_w = jnp.zeros((D_MODEL, LANES), F32)
    small_w = small_w.at[:, 0:DN_HEADS].set(w_i[:, c_b]).at[:, DN_HEADS:2 * DN_HEADS].set(w_i[:, c_a])
    small_b = jnp.zeros((LANES,), F32)
    small_b = small_b.at[0:DN_HEADS].set(b_i[c_b]).at[DN_HEADS:2 * DN_HEADS].set(b_i[c_a])
    w_a = jnp.concatenate([w_i[:, c_dn], small_w, w_i[:, c_q], w_i[:, c_k], w_i[:, c_v]], axis=1).astype(BF16)
    b_a = jnp.concatenate([b_i[c_dn], small_b, b_i[c_q], b_i[c_k], b_i[c_v]]).reshape(1, -1).astype(F32)
    alog = _pad_lanes(dn_a_log[li], DN_HEADS)
    dtb = _pad_lanes(dn_dt_bias[li], DN_HEADS)

    x2 = x.reshape(n, D_MODEL)
    tm_a = _tile(n, 512)
    dn_raw, ba, qk, vt = _in_proj(x2, _row(emb_ln_g), _row(emb_ln_b), w_a, b_a, alog, dtb,
                                  conv_w[li].astype(F32), tm_a, S)

    tb = _tile(S, 1024)
    ba3 = ba.reshape(B, S, LANES)
    gt4 = jnp.transpose(ba3[:, :, DN_HEADS:2 * DN_HEADS], (0, 2, 1)).reshape(B, DN_HEADS, 1, S)
    o_dn = _deltanet(dn_raw.reshape(B, S, DN_COLS), ba3, gt4, _row(dn_norm_w[li]), tb)

    slopes = jnp.asarray(2.0 ** (-8.0 * np.arange(1, DA_HEADS + 1) / DA_HEADS), dtype=F32)
    slopes = jnp.broadcast_to(slopes[:, None, None], (DA_HEADS, 1, LANES))
    o_da = _diffattn(qk.reshape(B, S, 2 * DA_QK_WIDTH), vt, _row(da_lq1[li]), _row(da_lk1[li]),
                      _row(da_lq2[li]), _row(da_lk2[li]), slopes, _row(da_subln_w[li]))

    wr = jnp.zeros((D_MODEL, LANES), F32)
    wr = wr.at[:, 0:N_EXPERTS].set(w_router_expert[li])
    wr = wr.at[:, ROUTER_GROUP_LANE0:ROUTER_GROUP_LANE0 + N_GROUPS].set(w_router_group[li])
    wr_hi = wr.astype(BF16)
    wr_lo = (wr - wr_hi.astype(F32)).astype(BF16)
    br = jnp.zeros((LANES,), F32).at[0:N_EXPERTS].set(b_router_expert[li])
    br = br.at[ROUTER_GROUP_LANE0:ROUTER_GROUP_LANE0 + N_GROUPS].set(b_router_group[li]).reshape(1, LANES)
    ws = (_row(emb_ln_g), _row(emb_ln_b),
          w_i[:, c_z].astype(BF16), _row(b_i[c_z]),
          w_i[:, c_g].astype(BF16), _row(b_i[c_g]),
          w_dn_o[li].astype(BF16), w_da_o[li].astype(BF16), w_out[li].astype(BF16),
          _row(ln1_g[li]), _row(ln1_b[li]), jnp.concatenate([wr_hi, wr_lo], axis=1), wr_hi, br,
          w_ple_gate[li].astype(BF16), _row(b_ple_gate[li]), w_ple_proj[li].astype(BF16))
    tm_c = _tile(n, 512)
    h1b, base, comb = _mixer(x2, o_dn.reshape(n, DN_WIDTH), o_da.reshape(n, DA_WIDTH),
                             p[li].reshape(n, PLE_DIM), ws, tm_c)

    wg = w_exp_gate[li].astype(BF16)
    wu = w_exp_up[li].astype(BF16)
    wd = w_exp_down[li].astype(BF16)
    tm_d = _tile(n, 1024)
    out = _moe(h1b, base, comb, wg, wu, wd, _row(ln2_g[li]), _row(ln2_b[li]), tm_d)
    return out.reshape(B, S, D_MODEL)
```

```python
import functools
import math

import jax
import jax.numpy as jnp
import numpy as np
from jax import lax
from jax.experimental import pallas as pl
from jax.experimental.pallas import tpu as pltpu

F32 = jnp.float32
BF16 = jnp.bfloat16

D_MODEL = 1024
PLE_DIM = 256
DN_HEADS = 4
DN_HEAD_DIM = 128
DN_WIDTH = DN_HEADS * DN_HEAD_DIM
CONV_WIDTH = 4
CHUNK = 64
DA_HEADS = 4
DA_HEAD_DIM = 64
DA_V_DIM = 2 * DA_HEAD_DIM
DA_QK_WIDTH = DA_HEADS * 2 * DA_HEAD_DIM
DA_WIDTH = DA_HEADS * DA_V_DIM
N_GROUPS = 4
EXPERTS_PER_GROUP = 4
N_EXPERTS = N_GROUPS * EXPERTS_PER_GROUP
EXPERT_FF = 256
IN_SIZES = (3 * DN_WIDTH, DN_WIDTH, DN_HEADS, DN_HEADS, DA_QK_WIDTH, DA_QK_WIDTH, DA_WIDTH, 2 * D_MODEL)
LN_EPS = 1e-5
RMS_EPS = 1e-6
DEPTH = 1
ALPHA = (2.0 * DEPTH) ** 0.25
LAM_INIT = 0.8 - 0.6 * math.exp(-0.3 * 0)

LANES = 128
VMEM_LIMIT = 56 * 1024 * 1024

NEG_INF = float("-inf")


def _cparams(sem, flags=None):
    return pltpu.CompilerParams(dimension_semantics=sem, vmem_limit_bytes=VMEM_LIMIT, flags=flags)


def _layer_norm(x, g, b):
    mu = jnp.mean(x, axis=-1, keepdims=True)
    xc = x - mu
    var = jnp.mean(xc * xc, axis=-1, keepdims=True)
    return xc * lax.rsqrt(var + LN_EPS) * g + b


def _sigmoid(x):
    return 1.0 / (1.0 + jnp.exp(-x))


def _silu(x):
    return x * _sigmoid(x)


def _dot(a, b):
    return jnp.dot(a, b, preferred_element_type=F32)


def _dot_nt(a, b):
    return lax.dot_general(a, b, (((1,), (1,)), ((), ())), preferred_element_type=F32)


DN_COLS = 3 * DN_WIDTH
DA_V_PAD = 16
DA_V_AUG = DA_V_DIM + DA_V_PAD
DA_TK = 512
DA_TQ = DA_TK
LOG2E = 1.4426950408889634


IN_PROJ_ROW_PARTS = 2
TAIL = 8


def _in_proj_kernel(tiles_per_seq, x_ref, g_ref, b_ref, w_ref, bias_ref, alog_ref, dtb_ref, cw_ref,
                    dn_ref, ba_ref, qk_ref, vt_ref, xe_ref):
    tm = x_ref.shape[0]
    assert vt_ref.shape[0] == 1 and tm == DA_TK
    n_parts = IN_PROJ_ROW_PARTS
    pm = tm // n_parts
    parts = [slice(r * pm, (r + 1) * pm) for r in range(n_parts)]
    c0 = DN_COLS + LANES
    qw = DA_QK_WIDTH

    def proj(hb, lo, hi):
        return _dot(hb, w_ref[:, lo:hi]) + bias_ref[:, lo:hi]

    @pl.when(pl.program_id(0) % tiles_per_seq == 0)
    def _():
        xe_ref[tm:tm + TAIL, :] = jnp.zeros((TAIL, DN_COLS), F32)

    xe_ref[0:TAIL, :] = xe_ref[tm:tm + TAIL, :]
    hbs = [_layer_norm(x_ref[rows, :], g_ref[...], b_ref[...]).astype(BF16) for rows in parts]
    for hb, rows in zip(hbs, parts):
        xe_ref[TAIL + rows.start:TAIL + rows.stop, :] = proj(hb, 0, DN_COLS)
    for hb, rows in zip(hbs, parts):
        qk_ref[rows, :qw] = (proj(hb, c0, c0 + qw) * (DA_HEAD_DIM ** -0.5 * LOG2E)).astype(BF16)
        qk_ref[rows, qw:] = proj(hb, c0 + qw, c0 + 2 * qw).astype(BF16)
    for hb, rows in zip(hbs, parts):
        v_t = proj(hb, c0 + 2 * qw, c0 + 3 * qw).T.astype(BF16)
        extra = (lax.broadcasted_iota(jnp.int32, (DA_V_PAD, pm), 0) == 0).astype(F32).astype(BF16)
        vt_ref[0, :, rows] = jnp.concatenate(
            [blk for hd in range(DA_HEADS) for blk in (v_t[hd * DA_V_DIM:(hd + 1) * DA_V_DIM], extra)], axis=0)
    for hb, rows in zip(hbs, parts):
        ba = proj(hb, DN_COLS, DN_COLS + LANES)
        lane = lax.broadcasted_iota(jnp.int32, (pm, LANES), 1)
        row = lax.broadcasted_iota(jnp.int32, (pm, LANES), 0)
        beta = _sigmoid(ba)
        xs = ba + dtb_ref[...]
        softplus = jnp.maximum(xs, 0.0) + jnp.log(1.0 + jnp.exp(-jnp.abs(xs)))
        g = -jnp.exp(alog_ref[...]) * softplus
        g = jnp.where((lane >= DN_HEADS) & (lane < 2 * DN_HEADS), g, 0.0)
        rin = row & (CHUNK - 1)
        k = 1
        while k < CHUNK:
            g = g + jnp.where(rin >= k, pltpu.roll(g, k, 0), 0.0)
            k *= 2
        ba_ref[rows, :] = jnp.where(lane < DN_HEADS, beta, g)

    cw = cw_ref[...]
    hd = DN_HEAD_DIM
    for rows in parts:
        for hh in range(3 * DN_HEADS):
            cols = slice(hh * hd, (hh + 1) * hd)
            r0 = TAIL + rows.start
            xh = xe_ref[r0:r0 + pm, cols] * cw[CONV_WIDTH - 1:CONV_WIDTH, cols]
            for j in range(CONV_WIDTH - 1):
                off = r0 - (CONV_WIDTH - 1) + j
                xh = xh + xe_ref[off:off + pm, cols] * cw[j:j + 1, cols]
            xh = _silu(xh)
            if hh < 2 * DN_HEADS:
                scale = (hd ** -0.5) if hh < DN_HEADS else 1.0
                xh = xh * lax.rsqrt(jnp.sum(xh * xh, axis=-1, keepdims=True) + RMS_EPS) * scale
            dn_ref[rows, cols] = xh


def _in_proj(x2, emb_g, emb_b, w_a, b_a, alog, dtb, conv_w, tm, seq):
    n = x2.shape[0]
    wcols = w_a.shape[1]
    assert seq % tm == 0
    const = lambda i: (0, 0)
    return pl.pallas_call(
        functools.partial(_in_proj_kernel, seq // tm),
        grid=(n // tm,),
        in_specs=[
            pl.BlockSpec((tm, D_MODEL), lambda i: (i, 0)),
            pl.BlockSpec((1, D_MODEL), const),
            pl.BlockSpec((1, D_MODEL), const),
            pl.BlockSpec((D_MODEL, wcols), const),
            pl.BlockSpec((1, wcols), const),
            pl.BlockSpec((1, LANES), const),
            pl.BlockSpec((1, LANES), const),
            pl.BlockSpec((CONV_WIDTH, DN_COLS), const),
        ],
        out_specs=[
            pl.BlockSpec((tm, DN_COLS), lambda i: (i, 0)),
            pl.BlockSpec((tm, LANES), lambda i: (i, 0)),
            pl.BlockSpec((tm, 2 * DA_QK_WIDTH), lambda i: (i, 0)),
            pl.BlockSpec((tm // DA_TK, DA_HEADS * DA_V_AUG, DA_TK), lambda i: (i, 0, 0)),
        ],
        out_shape=[
            jax.ShapeDtypeStruct((n, DN_COLS), F32),
            jax.ShapeDtypeStruct((n, LANES), F32),
            jax.ShapeDtypeStruct((n, 2 * DA_QK_WIDTH), BF16),
            jax.ShapeDtypeStruct((n // DA_TK, DA_HEADS * DA_V_AUG, DA_TK), BF16),
        ],
        scratch_shapes=[pltpu.VMEM((tm + 2 * TAIL, DN_COLS), F32)],
        compiler_params=_cparams(("arbitrary",)),
        name="in_proj",
    )(x2, emb_g, emb_b, w_a, b_a, alog, dtb, conv_w)


DN_GROUP_CHUNKS = 4


def _dn_kernel(x_ref, ba_ref, gt_ref, nw_ref, o_ref, state_ref):
    sblk = pl.program_id(1)
    tb = x_ref.shape[0]
    nck = tb // CHUNK
    hh = DN_HEADS
    hd = DN_HEAD_DIM

    @pl.when(sblk == 0)
    def _():
        state_ref[...] = jnp.zeros_like(state_ref)

    ba = ba_ref[...]
    ri = lax.broadcasted_iota(jnp.int32, (CHUNK, CHUNK), 0)
    ci = lax.broadcasted_iota(jnp.int32, (CHUNK, CHUNK), 1)
    causal = ri >= ci
    strict = ri > ci
    eye = jnp.where(ri == ci, 1.0, 0.0).astype(F32)
    nw = nw_ref[...]

    q_b, k_b, kb_b, rhs_uw, qg_b, kgt_b, decay, eg_last = {}, {}, {}, {}, {}, {}, {}, {}
    for h in range(hh):
        q = x_ref[:, h * hd:(h + 1) * hd]
        k = x_ref[:, (hh + h) * hd:(hh + h + 1) * hd]
        v = x_ref[:, (2 * hh + h) * hd:(2 * hh + h + 1) * hd]
        bcol = ba[:, h:h + 1]
        gcol = ba[:, hh + h:hh + h + 1]
        grow = gt_ref[h]
        egcol = jnp.exp(gcol)
        kb = k * bcol
        uw = jnp.concatenate([v * bcol, kb * egcol], axis=1).astype(BF16)
        qg = (q * egcol).astype(BF16)
        qb = q.astype(BF16)
        kbb = kb.astype(BF16)
        kk = k.astype(BF16)
        for c in range(nck):
            sl = slice(c * CHUNK, (c + 1) * CHUNK)
            gc = gcol[sl]
            g_last = gc[CHUNK - 1:CHUNK, :]
            q_b[h, c], k_b[h, c], kb_b[h, c] = qb[sl], kk[sl], kbb[sl]
            rhs_uw[h, c], qg_b[h, c] = uw[sl], qg[sl]
            kgt_b[h, c] = (k[sl] * jnp.exp(g_last - gc)).T.astype(BF16)
            decay[h, c] = jnp.exp(jnp.where(causal, gc - grow[:, sl], NEG_INF))
            eg_last[h, c] = jnp.exp(g_last)

    power, t, a_intra, u, w_b = {}, {}, {}, {}, {}
    n_rounds = int(math.log2(CHUNK)) - 1

    def local_stages(group):
        def s_nmat():
            for p_ in group:
                nmat = jnp.where(strict, -_dot_nt(kb_b[p_], k_b[p_]) * decay[p_], 0.0)
                power[p_] = nmat
                t[p_] = eye + nmat

        def s_intra():
            for p_ in group:
                a_intra[p_] = (_dot_nt(q_b[p_], k_b[p_]) * decay[p_]).astype(BF16)

        def s_square():
            for p_ in group:
                pb = power[p_].astype(BF16)
                power[p_] = _dot(pb, pb)

        def s_round(last):
            for p_ in group:
                pb = power[p_].astype(BF16)
                if last:
                    t[p_] = t[p_] + _dot(t[p_].astype(BF16), pb)
                else:
                    both = _dot(jnp.concatenate([power[p_], t[p_]], axis=0).astype(BF16), pb)
                    power[p_] = both[:CHUNK]
                    t[p_] = t[p_] + both[CHUNK:]

        def s_uw():
            for p_ in group:
                uw = _dot(t[p_].astype(BF16), rhs_uw[p_])
                u[p_] = uw[:, :hd]
                w_b[p_] = uw[:, hd:].astype(BF16)

        rounds = [functools.partial(s_round, r == n_rounds - 1) for r in range(n_rounds)]
        return [s_nmat, s_intra, s_square] + rounds + [s_uw]

    states = [state_ref[h] for h in range(hh)]
    ws = {}

    def recur_stages(chunks):
        def r_first(c):
            for h in range(hh):
                lhs = jnp.concatenate([w_b[h, c], qg_b[h, c]], axis=0)
                ws[h] = _dot(lhs, states[h].astype(BF16))

        def r_second(c):
            for h in range(hh):
                v_new = (u[h, c] - ws[h][:CHUNK]).astype(BF16)
                lhs = jnp.concatenate([a_intra[h, c], kgt_b[h, c]], axis=0)
                both = _dot(lhs, v_new)
                o_c = ws[h][CHUNK:] + both[:CHUNK]
                states[h] = states[h] * eg_last[h, c] + both[CHUNK:]
                o_n = o_c * lax.rsqrt(jnp.mean(o_c * o_c, axis=-1, keepdims=True) + RMS_EPS) * nw
                o_ref[c * CHUNK:(c + 1) * CHUNK, h * hd:(h + 1) * hd] = o_n

        return [functools.partial(f, c) for c in chunks for f in (r_first, r_second)]

    gsz = min(DN_GROUP_CHUNKS, nck)
    groups = [list(range(g0, g0 + gsz)) for g0 in range(0, nck, gsz)]
    pending = []
    for chunks in groups:
        local = local_stages([(h, c) for c in chunks for h in range(hh)])
        for k, stage in enumerate(local):
            stage()
            if pending:
                pending.pop(0)()
        while pending:
            pending.pop(0)()
        pending = recur_stages(chunks)
    while pending:
        pending.pop(0)()
    for h in range(hh):
        state_ref[h] = states[h]


def _deltanet(dn3, ba3, gt4, norm_w, tb):
    b, s, _ = dn3.shape
    hh = DN_HEADS
    return pl.pallas_call(
        _dn_kernel,
        grid=(b, s // tb),
        in_specs=[
            pl.BlockSpec((None, tb, DN_COLS), lambda bi, i: (bi, i, 0)),
            pl.BlockSpec((None, tb, LANES), lambda bi, i: (bi, i, 0)),
            pl.BlockSpec((None, hh, 1, tb), lambda bi, i: (bi, 0, 0, i)),
            pl.BlockSpec((1, LANES), lambda bi, i: (0, 0)),
        ],
        out_specs=pl.BlockSpec((None, tb, DN_WIDTH), lambda bi, i: (bi, i, 0)),
        out_shape=jax.ShapeDtypeStruct((b, s, DN_WIDTH), F32),
        scratch_shapes=[pltpu.VMEM((hh, DN_HEAD_DIM, DN_HEAD_DIM), F32)],
        compiler_params=_cparams(("parallel", "arbitrary")),
        name="deltanet",
    )(dn3, ba3, gt4, norm_w)


N_BIAS_LANES = 3
DA_UNROLL = 2


def _da_kernel(lq1_ref, lk1_ref, lq2_ref, lk2_ref, slope_ref, q_ref, k_ref, vt_ref, w_ref, o_ref,
                ka1_ref, ka2_ref, *scr):
    i = pl.program_id(2)
    tq = q_ref.shape[0]
    s_refs = ((scr[0], scr[1]), (scr[2], scr[3]))
    p_refs = ((scr[4], scr[5]), (scr[6], scr[7]))
    acc_refs = (scr[8], scr[9])
    st_ref = scr[10]
    kn_ref = scr[11]
    nkt, tk, _ = ka1_ref.shape
    half = DA_HEAD_DIM

    @pl.when(i == 0)
    def _():
        c = slope_ref[...] * LOG2E
        lane = lax.broadcasted_iota(jnp.int32, (tk, LANES), 1)
        row = lax.broadcasted_iota(jnp.int32, (tk, LANES), 0)

        def build(t, carry):
            kb = k_ref[pl.ds(pl.multiple_of(t * tk, tk), tk), :].astype(F32)
            bias = c * (t * tk + row).astype(F32)
            hi = bias.astype(BF16).astype(F32)
            rem = bias - hi
            mid = rem.astype(BF16).astype(F32)
            lo = rem - mid

            def feats(l0):
                return jnp.where(lane == l0, hi, jnp.where(lane == l0 + 1, mid,
                                                           jnp.where(lane == l0 + 2, lo, 0.0)))

            ka1_ref[t] = jnp.where(lane < half, kb, feats(half)).astype(BF16)
            ka2_ref[t] = jnp.where(lane >= half, kb, feats(0)).astype(BF16)
            sq = kb * kb
            n1 = jnp.max(jnp.sum(jnp.where(lane < half, sq, 0.0), axis=-1, keepdims=True), axis=0, keepdims=True)
            n2 = jnp.max(jnp.sum(jnp.where(lane >= half, sq, 0.0), axis=-1, keepdims=True), axis=0, keepdims=True)
            kn1, kn2 = carry
            return jnp.where(tile_lane == t, n1, kn1), jnp.where(tile_lane == t, n2, kn2)

        tile_lane = lax.broadcasted_iota(jnp.int32, (1, LANES), 1)
        zeros = jnp.zeros((1, LANES), F32)
        kn1, kn2 = lax.fori_loop(0, nkt, build, (zeros, zeros))
        kn_ref[0:1, :] = kn1
        kn_ref[1:2, :] = kn2

    qv = q_ref[...].astype(F32)
    lane_q = lax.broadcasted_iota(jnp.int32, qv.shape, 1)
    q1 = jnp.where(lane_q < half, qv, jnp.where(lane_q < half + N_BIAS_LANES, 1.0, 0.0))
    q2 = jnp.where(lane_q >= half, qv, jnp.where(lane_q < N_BIAS_LANES, 1.0, 0.0))
    qts = (q1.T.astype(BF16), q2.T.astype(BF16))
    kas = (ka1_ref, ka2_ref)

    SKIP_BELOW = -170.0
    NORM_SLACK = 1.01
    qsq = qv * qv
    qn = [jnp.sqrt(jnp.max(jnp.sum(jnp.where(msk, qsq, 0.0), axis=-1, keepdims=True), axis=0, keepdims=True))
          for msk in (lane_q < half, lane_q >= half)]
    tile_lane = lax.broadcasted_iota(jnp.int32, (1, LANES), 1)
    gap = ((i - tile_lane - 1) * tk + 1).astype(F32)
    c_row = slope_ref[...] * LOG2E
    bound = None
    for mp in range(2):
        kn = jnp.sqrt(kn_ref[mp:mp + 1, :])
        kd = jnp.max(jnp.where(tile_lane == i, kn, 0.0), axis=-1, keepdims=True)
        b_mp = qn[mp] * (kn + kd) * NORM_SLACK - c_row * gap
        bound = b_mp if bound is None else jnp.maximum(bound, b_mp)
    skippable = (bound < SKIP_BELOW) & (tile_lane < i)
    first_keep = jnp.min(jnp.where(skippable, float(LANES), tile_lane.astype(F32)), axis=-1, keepdims=True)
    base = first_keep.astype(jnp.int32)[0, 0]
    cnt = i - base

    M_ROW, A_ROW = 0, 2
    CH = 128
    SUB = 8
    V_AFTER_CHUNK = DA_TK // CH - 1

    def row(mp, r):
        return st_ref[mp, r:r + 1, :]

    def fold(x, op):
        return op(x.reshape(CH // SUB, SUB, tq), axis=0)

    def step(t, slot, s_mode, do_v=True, do_e=True):
        o = 1 - slot
        a_prevs = [row(mp, A_ROW + o) for mp in range(2)]
        for mp in range(2):
            m_t = row(mp, M_ROW + slot)
            tmax = jnp.full((SUB, tq), NEG_INF, F32)
            for c in range(tk // CH):
                rows = slice(c * CH, (c + 1) * CH)
                if do_e:
                    pc = jnp.exp2(s_refs[slot][mp][rows, :] - m_t)
                    p_refs[slot][mp][rows, :] = pc.astype(BF16)
                if s_mode is not None:
                    sc = _dot(kas[mp][base + t + 1, rows, :], qts[mp])
                    if s_mode != "plain":
                        krow = lax.broadcasted_iota(jnp.int32, (CH, tq), 0) + (c * CH + s_mode)
                        qcol = lax.broadcasted_iota(jnp.int32, (CH, tq), 1)
                        sc = jnp.where(krow <= qcol, sc, NEG_INF)
                    s_refs[o][mp][rows, :] = sc
                    tmax = jnp.maximum(tmax, fold(sc, jnp.max))
                if do_v and c == V_AFTER_CHUNK:
                    vt = vt_ref[base + jnp.maximum(t - 1, 0)]
                    acc_refs[mp][...] = a_prevs[mp] * acc_refs[mp][...] + _dot(vt, p_refs[o][mp][...])
            if s_mode is not None:
                m_new = jnp.maximum(m_t, jnp.max(tmax, axis=0, keepdims=True))
                st_ref[mp, M_ROW + o:M_ROW + o + 1, :] = m_new
                st_ref[mp, A_ROW + o:A_ROW + o + 1, :] = jnp.exp2(m_t - m_new)

    for mp in range(2):
        p_refs[1][mp][...] = jnp.zeros((tk, tq), BF16)
        acc_refs[mp][...] = jnp.zeros((DA_V_AUG, tq), F32)
        st_ref[mp, M_ROW + 1:M_ROW + 2, :] = jnp.full((1, tq), -1e30, F32)
        st_ref[mp, A_ROW + 1:A_ROW + 2, :] = jnp.ones((1, tq), F32)

    @pl.when(cnt == 0)
    def _():
        step(-1, 1, 0, do_v=False, do_e=False)
        step(0, 0, None)
        step(1, 1, None, do_e=False)

    @pl.when(cnt > 0)
    def _():
        step(-1, 1, "plain", do_v=False, do_e=False)

    unroll = DA_UNROLL

    def body(it, carry):
        for k in range(unroll):
            step(unroll * it + k, k & 1, "plain")
        return carry

    n_plain = jnp.maximum(cnt - 1, 0)
    lax.fori_loop(0, n_plain // unroll, body, 0)

    for rem in range(unroll):
        @pl.when((cnt > 0) & (n_plain % unroll == rem))
        def _(rem=rem):
            for k in range(rem):
                step(cnt - 1 - rem + k, k & 1, "plain")
            step(cnt - 1, rem & 1, 0)
            step(cnt, (rem + 1) & 1, None)
            step(cnt + 1, rem & 1, None, do_e=False)

    l1 = acc_refs[0][DA_V_DIM:DA_V_DIM + 1, :]
    l2 = acc_refs[1][DA_V_DIM:DA_V_DIM + 1, :]
    a1 = acc_refs[0][0:DA_V_DIM, :]
    a2 = acc_refs[1][0:DA_V_DIM, :]

    lam = (jnp.exp(jnp.sum(lq1_ref[...] * lk1_ref[...], axis=-1, keepdims=True))
           - jnp.exp(jnp.sum(lq2_ref[...] * lk2_ref[...], axis=-1, keepdims=True)) + LAM_INIT)
    o_t = a1 / l1 - lam * (a2 / l2)
    o_t = o_t * lax.rsqrt(jnp.mean(o_t * o_t, axis=0, keepdims=True) + RMS_EPS) * (1.0 - LAM_INIT)
    o_ref[...] = (o_t.T * w_ref[...]).astype(BF16)


def _diffattn(qk3, vt3, lq1, lk1, lq2, lk2, slopes, subln_w):
    b, s, _ = qk3.shape
    hh = DA_HEADS
    tq, tk = DA_TQ, DA_TK
    assert tq == tk and s % tq == 0
    nkt = s // tk
    c2 = lambda bi, h, i: (0, 0)
    return pl.pallas_call(
        _da_kernel,
        grid=(b, hh, s // tq),
        in_specs=[
            pl.BlockSpec((1, DA_HEAD_DIM), c2),
            pl.BlockSpec((1, DA_HEAD_DIM), c2),
            pl.BlockSpec((1, DA_HEAD_DIM), c2),
            pl.BlockSpec((1, DA_HEAD_DIM), c2),
            pl.BlockSpec((None, 1, LANES), lambda bi, h, i: (h, 0, 0)),
            pl.BlockSpec((None, tq, LANES), lambda bi, h, i: (bi, i, h)),
            pl.BlockSpec((None, s, LANES), lambda bi, h, i: (bi, 0, hh + h)),
            pl.BlockSpec((nkt, DA_V_AUG, tk), lambda bi, h, i: (bi, h, 0)),
            pl.BlockSpec((1, DA_V_DIM), c2),
        ],
        out_specs=pl.BlockSpec((None, tq, LANES), lambda bi, h, i: (bi, i, h)),
        out_shape=jax.ShapeDtypeStruct((b, s, DA_WIDTH), BF16),
        scratch_shapes=[
            pltpu.VMEM((nkt, tk, LANES), BF16),
            pltpu.VMEM((nkt, tk, LANES), BF16),
        ] + [pltpu.VMEM((tk, tq), F32)] * 4 + [pltpu.VMEM((tk, tq), BF16)] * 4 + [
            pltpu.VMEM((DA_V_AUG, tq), F32),
            pltpu.VMEM((DA_V_AUG, tq), F32),
            pltpu.VMEM((2, 8, tq), F32),
            pltpu.VMEM((8, LANES), F32),
        ],
        compiler_params=_cparams(("parallel", "parallel", "arbitrary")),
        name="diffattn",
    )(lq1, lk1, lq2, lk2, slopes, qk3, qk3, vt3, subln_w)


ROUTER_GROUP_LANE0 = N_EXPERTS
MIXER_ROW_PARTS = 2


def _mixer_kernel(x_ref, odn_ref, oda_ref, p_ref, eg_ref, eb_ref, wz_ref, bz_ref, wg_ref, bg_ref,
                  wdn_ref, wda_ref, wout_ref, g1_ref, b1_ref, wr_cat_ref, wr_hi_ref, br_ref,
                  wpg_ref, bpg_ref, wpp_ref, h1b_ref, base_ref, logits_ref):
    n_parts = MIXER_ROW_PARTS
    pm = x_ref.shape[0] // n_parts
    parts = [slice(r * pm, (r + 1) * pm) for r in range(n_parts)]
    st = [dict() for _ in parts]
    for s, rows in zip(st, parts):
        s["h"] = _layer_norm(x_ref[rows, :], eg_ref[...], eb_ref[...])
        s["hb"] = s["h"].astype(BF16)
    for s, rows in zip(st, parts):
        s["z"] = _dot(s["hb"], wz_ref[...]) + bz_ref[...]
        s["gl"] = _dot(s["hb"], wg_ref[...]) + bg_ref[...]
    for s, rows in zip(st, parts):
        s["y_dn"] = _dot((odn_ref[rows, :] * _silu(s["z"])).astype(BF16), wdn_ref[...])
        s["y_da"] = _dot(oda_ref[rows, :], wda_ref[...])
    for s, rows in zip(st, parts):
        gates = _sigmoid(s["gl"])
        mixed = gates[:, :D_MODEL] * s["y_dn"] + gates[:, D_MODEL:] * s["y_da"]
        s["mix"] = _dot(mixed.astype(BF16), wout_ref[...])
    for s, rows in zip(st, parts):
        s["h1"] = _layer_norm(ALPHA * s["h"] + s["mix"], g1_ref[...], b1_ref[...])
        s["h1b"] = s["h1"].astype(BF16)
        h1b_ref[rows, :] = s["h1b"]
    for s, rows in zip(st, parts):
        ple = (_sigmoid(_dot(s["h1b"], wpg_ref[...]) + bpg_ref[...])
               * _dot(p_ref[rows, :].astype(BF16), wpp_ref[...]))
        base_ref[rows, :] = ALPHA * s["h1"] + ple
        h1_lo = (s["h1"] - s["h1b"].astype(F32)).astype(BF16)
        both = _dot(s["h1b"], wr_cat_ref[...])
        s["logits"] = (both[:, :LANES] + both[:, LANES:] + _dot(h1_lo, wr_hi_ref[...]) + br_ref[...])
    for s, rows in zip(st, parts):
        logits_ref[rows, :] = s["logits"]


def _route(logits):
    tm = logits.shape[0]
    lane = lax.broadcasted_iota(jnp.int32, (tm, LANES), 1).astype(F32)
    big = float(LANES)
    is_g = (lane >= ROUTER_GROUP_LANE0) & (lane < ROUTER_GROUP_LANE0 + N_GROUPS)
    gl = jnp.where(is_g, logits, NEG_INF)
    gmax = jnp.max(gl, axis=-1, keepdims=True)
    gsum = jnp.sum(jnp.exp(gl - gmax), axis=-1, keepdims=True)
    g_val = 1.0 / gsum
    g_idx = jnp.min(jnp.where(gl == gmax, lane, big), axis=-1, keepdims=True) - ROUTER_GROUP_LANE0
    e_lo = g_idx * EXPERTS_PER_GROUP
    in_g = (lane >= e_lo) & (lane < e_lo + EXPERTS_PER_GROUP)
    el = jnp.where(in_g, logits, NEG_INF)
    emax = jnp.max(el, axis=-1, keepdims=True)
    ee = jnp.exp(el - emax)
    e_prob = ee / jnp.sum(ee, axis=-1, keepdims=True)
    p1 = jnp.max(e_prob, axis=-1, keepdims=True)
    i1 = jnp.min(jnp.where(in_g & (e_prob == p1), lane, big), axis=-1, keepdims=True)
    rest = jnp.where(in_g & (lane != i1), e_prob, -1.0)
    p2 = jnp.max(rest, axis=-1, keepdims=True)
    i2 = jnp.min(jnp.where(rest == p2, lane, big), axis=-1, keepdims=True)
    denom = p1 + p2
    return (jnp.where(lane == i1, p1 / denom * g_val, 0.0)
            + jnp.where(lane == i2, p2 / denom * g_val, 0.0))


def _mixer(x2, odn2, oda2, p2, ws, tm):
    n = x2.shape[0]
    row = lambda i: (i, 0)
    const = lambda i: (0, 0)

    def cspec(a):
        return pl.BlockSpec(a.shape, const, pipeline_mode=pl.Buffered(1))

    (eg, eb, wz, bz, wg, bg, wdn, wda, wout, g1, b1, wr_cat, wr_hi, br, wpg, bpg, wpp) = ws
    return pl.pallas_call(
        _mixer_kernel,
        grid=(n // tm,),
        in_specs=[
            pl.BlockSpec((tm, D_MODEL), row),
            pl.BlockSpec((tm, DN_WIDTH), row),
            pl.BlockSpec((tm, DA_WIDTH), row),
            pl.BlockSpec((tm, PLE_DIM), row),
        ] + [cspec(a) for a in ws],
        out_specs=[
            pl.BlockSpec((tm, D_MODEL), row),
            pl.BlockSpec((tm, D_MODEL), row),
            pl.BlockSpec((tm, LANES), row),
        ],
        out_shape=[
            jax.ShapeDtypeStruct((n, D_MODEL), BF16),
            jax.ShapeDtypeStruct((n, D_MODEL), F32),
            jax.ShapeDtypeStruct((n, LANES), F32),
        ],
        compiler_params=_cparams(("parallel",)),
        name="mixer",
    )(x2, odn2, oda2, p2, *ws)


def _moe_kernel(h_ref, base_ref, logits_ref, wg_ref, wu_ref, wd_ref, g2_ref, b2_ref, o_ref, acc_ref):
    g = pl.program_id(1)

    @pl.when(g == 0)
    def _():
        acc_ref[...] = base_ref[...]

    comb = _route(logits_ref[...])
    lane = lax.broadcasted_iota(jnp.int32, comb.shape, 1)
    h = h_ref[...]
    cols = []
    for e in range(EXPERTS_PER_GROUP):
        c = jnp.sum(jnp.where(lane == g * EXPERTS_PER_GROUP + e, comb, 0.0), axis=-1, keepdims=True)
        hid = _silu(_dot(h, wg_ref[e])) * _dot(h, wu_ref[e])
        cols.append((hid * c).astype(BF16))
    hid_all = jnp.concatenate(cols, axis=1)
    wd = wd_ref[...].reshape(EXPERTS_PER_GROUP * EXPERT_FF, D_MODEL)
    acc_ref[...] += _dot(hid_all, wd)

    @pl.when(g == N_GROUPS - 1)
    def _():
        o_ref[...] = _layer_norm(acc_ref[...], g2_ref[...], b2_ref[...])


def _moe(h1b, base, logits, wg, wu, wd, g2, b2, tm):
    n = h1b.shape[0]
    row = lambda i, g: (i, 0)
    return pl.pallas_call(
        _moe_kernel,
        grid=(n // tm, N_GROUPS),
        in_specs=[
            pl.BlockSpec((tm, D_MODEL), row),
            pl.BlockSpec((tm, D_MODEL), row),
            pl.BlockSpec((tm, LANES), row),
            pl.BlockSpec((EXPERTS_PER_GROUP, D_MODEL, EXPERT_FF), lambda i, g: (g, 0, 0)),
            pl.BlockSpec((EXPERTS_PER_GROUP, D_MODEL, EXPERT_FF), lambda i, g: (g, 0, 0)),
            pl.BlockSpec((EXPERTS_PER_GROUP, EXPERT_FF, D_MODEL), lambda i, g: (g, 0, 0)),
            pl.BlockSpec((1, D_MODEL), lambda i, g: (0, 0)),
            pl.BlockSpec((1, D_MODEL), lambda i, g: (0, 0)),
        ],
        out_specs=pl.BlockSpec((tm, D_MODEL), row),
        out_shape=jax.ShapeDtypeStruct((n, D_MODEL), F32),
        scratch_shapes=[pltpu.VMEM((tm, D_MODEL), F32)],
        compiler_params=_cparams(("parallel", "arbitrary")),
        name="moe",
    )(h1b, base, logits, wg, wu, wd, g2, b2)


def _row(v):
    return v.reshape(1, -1).astype(F32)


def _pad_lanes(v, offset=0):
    out = jnp.zeros((1, LANES), F32)
    return lax.dynamic_update_slice(out, v.reshape(1, -1).astype(F32), (0, offset))


def _tile(n, pref):
    t = min(pref, n)
    while n % t:
        t //= 2
    return t


def kernel(x, p, emb_ln_g, emb_ln_b, w_in, b_in, conv_w, dn_a_log, dn_dt_bias, dn_norm_w, w_dn_o,
           da_lq1, da_lk1, da_lq2, da_lk2, da_subln_w, w_da_o, w_out, ln1_g, ln1_b,
           w_router_group, b_router_group, w_router_expert, b_router_expert,
           w_exp_gate, w_exp_up, w_exp_down, w_ple_gate, b_ple_gate, w_ple_proj, ln2_g, ln2_b):
    B, S, _ = x.shape
    n = B * S
    assert S % CHUNK == 0
    li = 0
    offs = np.concatenate([[0], np.cumsum(IN_SIZES)]).astype(int)
    c_dn, c_z, c_b, c_a, c_q, c_k, c_v, c_g = [slice(int(offs[j]), int(offs[j + 1])) for j in range(8)]
    w_i, b_i = w_in[li], b_in[li]

    small_w = jnp.zeros((D_MODEL, LANES), F32)
    small_w = small_w.at[:, 0:DN_HEADS].set(w_i[:, c_b]).at[:, DN_HEADS:2 * DN_HEADS].set(w_i[:, c_a])
    small_b = jnp.zeros((LANES,), F32)
    small_b = small_b.at[0:DN_HEADS].set(b_i[c_b]).at[DN_HEADS:2 * DN_HEADS].set(b_i[c_a])
    w_a = jnp.concatenate([w_i[:, c_dn], small_w, w_i[:, c_q], w_i[:, c_k], w_i[:, c_v]], axis=1).astype(BF16)
    b_a = jnp.concatenate([b_i[c_dn], small_b, b_i[c_q], b_i[c_k], b_i[c_v]]).reshape(1, -1).astype(F32)
    alog = _pad_lanes(dn_a_log[li], DN_HEADS)
    dtb = _pad_lanes(dn_dt_bias[li], DN_HEADS)

    x2 = x.reshape(n, D_MODEL)
    tm_a = _tile(n, 512)
    dn_raw, ba, qk, vt = _in_proj(x2, _row(emb_ln_g), _row(emb_ln_b), w_a, b_a, alog, dtb,
                                  conv_w[li].astype(F32), tm_a, S)

    tb = _tile(S, 1024)
    ba3 = ba.reshape(B, S, LANES)
    gt4 = jnp.transpose(ba3[:, :, DN_HEADS:2 * DN_HEADS], (0, 2, 1)).reshape(B, DN_HEADS, 1, S)
    o_dn = _deltanet(dn_raw.reshape(B, S, DN_COLS), ba3, gt4, _row(dn_norm_w[li]), tb)

    slopes = jnp.asarray(2.0 ** (-8.0 * np.arange(1, DA_HEADS + 1) / DA_HEADS), dtype=F32)
    slopes = jnp.broadcast_to(slopes[:, None, None], (DA_HEADS, 1, LANES))
    o_da = _diffattn(qk.reshape(B, S, 2 * DA_QK_WIDTH), vt, _row(da_lq1[li]), _row(da_lk1[li]),
                      _row(da_lq2[li]), _row(da_lk2[li]), slopes, _row(da_subln_w[li]))

    wr = jnp.zeros((D_MODEL, LANES), F32)
    wr = wr.at[:, 0:N_EXPERTS].set(w_router_expert[li])
    wr = wr.at[:, ROUTER_GROUP_LANE0:ROUTER_GROUP_LANE0 + N_GROUPS].set(w_router_group[li])
    wr_hi = wr.astype(BF16)
    wr_lo = (wr - wr_hi.astype(F32)).astype(BF16)
    br = jnp.zeros((LANES,), F32).at[0:N_EXPERTS].set(b_router_expert[li])
    br = br.at[ROUTER_GROUP_LANE0:ROUTER_GROUP_LANE0 + N_GROUPS].set(b_router_group[li]).reshape(1, LANES)
    ws = (_row(emb_ln_g), _row(emb_ln_b),
          w_i[:, c_z].astype(BF16), _row(b_i[c_z]),
          w_i[:, c_g].astype(BF16), _row(b_i[c_g]),
          w_dn_o[li].astype(BF16), w_da_o[li].astype(BF16), w_out[li].astype(BF16),
          _row(ln1_g[li]), _row(ln1_b[li]), jnp.concatenate([wr_hi, wr_lo], axis=1), wr_hi, br,
          w_ple_gate[li].astype(BF16), _row(b_ple_gate[li]), w_ple_proj[li].astype(BF16))
    tm_c = _tile(n, 512)
    h1b, base, logits = _mixer(x2, o_dn.reshape(n, DN_WIDTH), o_da.reshape(n, DA_WIDTH),
                             p[li].reshape(n, PLE_DIM), ws, tm_c)

    wg = w_exp_gate[li].astype(BF16)
    wu = w_exp_up[li].astype(BF16)
    wd = w_exp_down[li].astype(BF16)
    tm_d = _tile(n, 1024)
    out = _moe(h1b, base, logits, wg, wu, wd, _row(ln2_g[li]), _row(ln2_b[li]), tm_d)
    return out.reshape(B, S, D_MODEL)
```

```python
import functools
import math

import jax
import jax.numpy as jnp
import numpy as np
from jax import lax
from jax.experimental import pallas as pl
from jax.experimental.pallas import tpu as pltpu

F32 = jnp.float32
BF16 = jnp.bfloat16

D_MODEL = 1024
PLE_DIM = 256
DN_HEADS = 4
DN_HEAD_DIM = 128
DN_WIDTH = DN_HEADS * DN_HEAD_DIM
CONV_WIDTH = 4
CHUNK = 64
DA_HEADS = 4
DA_HEAD_DIM = 64
DA_V_DIM = 2 * DA_HEAD_DIM
DA_QK_WIDTH = DA_HEADS * 2 * DA_HEAD_DIM
DA_WIDTH = DA_HEADS * DA_V_DIM
N_GROUPS = 4
EXPERTS_PER_GROUP = 4
N_EXPERTS = N_GROUPS * EXPERTS_PER_GROUP
EXPERT_FF = 256
IN_SIZES = (3 * DN_WIDTH, DN_WIDTH, DN_HEADS, DN_HEADS, DA_QK_WIDTH, DA_QK_WIDTH, DA_WIDTH, 2 * D_MODEL)
LN_EPS = 1e-5
RMS_EPS = 1e-6
DEPTH = 1
ALPHA = (2.0 * DEPTH) ** 0.25
LAM_INIT = 0.8 - 0.6 * math.exp(-0.3 * 0)

LANES = 128
VMEM_LIMIT = 56 * 1024 * 1024

NEG_INF = float("-inf")


def _cparams(sem, flags=None):
    return pltpu.CompilerParams(dimension_semantics=sem, vmem_limit_bytes=VMEM_LIMIT, flags=flags)


def _layer_norm(x, g, b):
    mu = jnp.mean(x, axis=-1, keepdims=True)
    xc = x - mu
    var = jnp.mean(xc * xc, axis=-1, keepdims=True)
    return xc * lax.rsqrt(var + LN_EPS) * g + b


def _sigmoid(x):
    return 1.0 / (1.0 + jnp.exp2(x * (-LOG2E)))


def _silu(x):
    return x * _sigmoid(x)


def _dot(a, b):
    return jnp.dot(a, b, preferred_element_type=F32)


def _dot_nt(a, b):
    return lax.dot_general(a, b, (((1,), (1,)), ((), ())), preferred_element_type=F32)


DN_COLS = 3 * DN_WIDTH
DA_V_PAD = 16
DA_V_AUG = DA_V_DIM + DA_V_PAD
DA_TK = 512
DA_TQ = DA_TK
LOG2E = 1.4426950408889634


IN_PROJ_ROW_PARTS = 2
TAIL = 8
BAT_ROWS = 8


def _in_proj_kernel(tiles_per_seq, x_ref, g_ref, b_ref, w_ref, bias_ref, alog_ref, dtb_ref, cw_ref,
                    dn_ref, ba_ref, bat_ref, qk_ref, vt_ref, xe_ref):
    tm = x_ref.shape[0]
    assert vt_ref.shape[0] == 1 and tm == DA_TK
    n_parts = IN_PROJ_ROW_PARTS
    pm = tm // n_parts
    parts = [slice(r * pm, (r + 1) * pm) for r in range(n_parts)]
    c0 = DN_COLS + LANES
    qw = DA_QK_WIDTH

    def proj(hb, lo, hi):
        return _dot(hb, w_ref[:, lo:hi]) + bias_ref[:, lo:hi]

    @pl.when(pl.program_id(0) % tiles_per_seq == 0)
    def _():
        xe_ref[tm:tm + TAIL, :] = jnp.zeros((TAIL, DN_COLS), F32)

    xe_ref[0:TAIL, :] = xe_ref[tm:tm + TAIL, :]
    hbs = [_layer_norm(x_ref[rows, :], g_ref[...], b_ref[...]).astype(BF16) for rows in parts]
    for hb, rows in zip(hbs, parts):
        xe_ref[TAIL + rows.start:TAIL + rows.stop, :] = proj(hb, 0, DN_COLS)
    for hb, rows in zip(hbs, parts):
        qk_ref[rows, :qw] = (proj(hb, c0, c0 + qw) * (DA_HEAD_DIM ** -0.5 * LOG2E)).astype(BF16)
        qk_ref[rows, qw:] = proj(hb, c0 + qw, c0 + 2 * qw).astype(BF16)
    for hb, rows in zip(hbs, parts):
        v_t = proj(hb, c0 + 2 * qw, c0 + 3 * qw).T.astype(BF16)
        extra = (lax.broadcasted_iota(jnp.int32, (DA_V_PAD, pm), 0) == 0).astype(F32).astype(BF16)
        vt_ref[0, :, rows] = jnp.concatenate(
            [blk for hd in range(DA_HEADS) for blk in (v_t[hd * DA_V_DIM:(hd + 1) * DA_V_DIM], extra)], axis=0)
    for hb, rows in zip(hbs, parts):
        ba = proj(hb, DN_COLS, DN_COLS + LANES)
        lane = lax.broadcasted_iota(jnp.int32, (pm, LANES), 1)
        row = lax.broadcasted_iota(jnp.int32, (pm, LANES), 0)
        beta = _sigmoid(ba)
        xs = ba + dtb_ref[...]
        softplus = jnp.maximum(xs, 0.0) + jnp.log(1.0 + jnp.exp(-jnp.abs(xs)))
        g = -jnp.exp(alog_ref[...]) * softplus
        g = jnp.where((lane >= DN_HEADS) & (lane < 2 * DN_HEADS), g, 0.0)
        rin = row & (CHUNK - 1)
        k = 1
        while k < CHUNK:
            g = g + jnp.where(rin >= k, pltpu.roll(g, k, 0), 0.0)
            k *= 2
        ba_out = jnp.where(lane < DN_HEADS, beta, g)
        ba_ref[rows, :] = ba_out
        bat_ref[:, rows] = ba_out.T[0:BAT_ROWS, :]

    cw = cw_ref[...]
    hd = DN_HEAD_DIM
    for rows in parts:
        for hh in range(3 * DN_HEADS):
            cols = slice(hh * hd, (hh + 1) * hd)
            r0 = TAIL + rows.start
            xh = xe_ref[r0:r0 + pm, cols] * cw[CONV_WIDTH - 1:CONV_WIDTH, cols]
            for j in range(CONV_WIDTH - 1):
                off = r0 - (CONV_WIDTH - 1) + j
                xh = xh + xe_ref[off:off + pm, cols] * cw[j:j + 1, cols]
            xh = _silu(xh)
            if hh < 2 * DN_HEADS:
                scale = (hd ** -0.5) if hh < DN_HEADS else 1.0
                xh = xh * lax.rsqrt(jnp.sum(xh * xh, axis=-1, keepdims=True) + RMS_EPS) * scale
            dn_ref[rows, cols] = xh


def _in_proj(x2, emb_g, emb_b, w_a, b_a, alog, dtb, conv_w, tm, seq):
    n = x2.shape[0]
    wcols = w_a.shape[1]
    assert seq % tm == 0
    const = lambda i: (0, 0)
    return pl.pallas_call(
        functools.partial(_in_proj_kernel, seq // tm),
        grid=(n // tm,),
        in_specs=[
            pl.BlockSpec((tm, D_MODEL), lambda i: (i, 0)),
            pl.BlockSpec((1, D_MODEL), const),
            pl.BlockSpec((1, D_MODEL), const),
            pl.BlockSpec((D_MODEL, wcols), const),
            pl.BlockSpec((1, wcols), const),
            pl.BlockSpec((1, LANES), const),
            pl.BlockSpec((1, LANES), const),
            pl.BlockSpec((CONV_WIDTH, DN_COLS), const),
        ],
        out_specs=[
            pl.BlockSpec((tm, DN_COLS), lambda i: (i, 0)),
            pl.BlockSpec((tm, LANES), lambda i: (i, 0)),
            pl.BlockSpec((BAT_ROWS, tm), lambda i: (0, i)),
            pl.BlockSpec((tm, 2 * DA_QK_WIDTH), lambda i: (i, 0)),
            pl.BlockSpec((tm // DA_TK, DA_HEADS * DA_V_AUG, DA_TK), lambda i: (i, 0, 0)),
        ],
        out_shape=[
            jax.ShapeDtypeStruct((n, DN_COLS), F32),
            jax.ShapeDtypeStruct((n, LANES), F32),
            jax.ShapeDtypeStruct((BAT_ROWS, n), F32),
            jax.ShapeDtypeStruct((n, 2 * DA_QK_WIDTH), BF16),
            jax.ShapeDtypeStruct((n // DA_TK, DA_HEADS * DA_V_AUG, DA_TK), BF16),
        ],
        scratch_shapes=[pltpu.VMEM((tm + 2 * TAIL, DN_COLS), F32)],
        compiler_params=_cparams(("arbitrary",)),
        name="in_proj",
    )(x2, emb_g, emb_b, w_a, b_a, alog, dtb, conv_w)


DN_GROUP_CHUNKS = 4


def _dn_kernel(x_ref, ba_ref, gt_ref, nw_ref, o_ref, state_ref):
    sblk = pl.program_id(1)
    tb = x_ref.shape[0]
    nck = tb // CHUNK
    hh = DN_HEADS
    hd = DN_HEAD_DIM

    @pl.when(sblk == 0)
    def _():
        state_ref[...] = jnp.zeros_like(state_ref)

    ba = ba_ref[...]
    ri = lax.broadcasted_iota(jnp.int32, (CHUNK, CHUNK), 0)
    ci = lax.broadcasted_iota(jnp.int32, (CHUNK, CHUNK), 1)
    causal = ri >= ci
    strict = ri > ci
    eye = jnp.where(ri == ci, 1.0, 0.0).astype(F32)
    nw = nw_ref[...]

    q_b, k_b, kb_b, rhs_uw, qg_b, kgt_b, decay, eg_last = {}, {}, {}, {}, {}, {}, {}, {}
    for h in range(hh):
        q = x_ref[:, h * hd:(h + 1) * hd]
        k = x_ref[:, (hh + h) * hd:(hh + h + 1) * hd]
        v = x_ref[:, (2 * hh + h) * hd:(2 * hh + h + 1) * hd]
        bcol = ba[:, h:h + 1]
        gcol = ba[:, hh + h:hh + h + 1]
        grow = gt_ref[hh + h:hh + h + 1, :]
        egcol = jnp.exp(gcol)
        kb = k * bcol
        uw = jnp.concatenate([v * bcol, kb * egcol], axis=1).astype(BF16)
        qg = (q * egcol).astype(BF16)
        qb = q.astype(BF16)
        kbb = kb.astype(BF16)
        kk = k.astype(BF16)
        for c in range(nck):
            sl = slice(c * CHUNK, (c + 1) * CHUNK)
            gc = gcol[sl]
            g_last = gc[CHUNK - 1:CHUNK, :]
            q_b[h, c], k_b[h, c], kb_b[h, c] = qb[sl], kk[sl], kbb[sl]
            rhs_uw[h, c], qg_b[h, c] = uw[sl], qg[sl]
            kgt_b[h, c] = (k[sl] * jnp.exp(g_last - gc)).T.astype(BF16)
            decay[h, c] = jnp.exp(jnp.where(causal, gc - grow[:, sl], NEG_INF))
            eg_last[h, c] = jnp.exp(g_last)

    power, t, a_intra, u, w_b = {}, {}, {}, {}, {}
    n_rounds = int(math.log2(CHUNK)) - 1

    def local_stages(group):
        def s_nmat():
            for p_ in group:
                nmat = jnp.where(strict, -_dot_nt(kb_b[p_], k_b[p_]) * decay[p_], 0.0)
                power[p_] = nmat
                t[p_] = eye + nmat

        def s_intra():
            for p_ in group:
                a_intra[p_] = (_dot_nt(q_b[p_], k_b[p_]) * decay[p_]).astype(BF16)

        def s_square():
            for p_ in group:
                pb = power[p_].astype(BF16)
                power[p_] = _dot(pb, pb)

        def s_round(last):
            for p_ in group:
                pb = power[p_].astype(BF16)
                if last:
                    t[p_] = t[p_] + _dot(t[p_].astype(BF16), pb)
                else:
                    both = _dot(jnp.concatenate([power[p_], t[p_]], axis=0).astype(BF16), pb)
                    power[p_] = both[:CHUNK]
                    t[p_] = t[p_] + both[CHUNK:]

        def s_uw():
            for p_ in group:
                uw = _dot(t[p_].astype(BF16), rhs_uw[p_])
                u[p_] = uw[:, :hd]
                w_b[p_] = uw[:, hd:].astype(BF16)

        rounds = [functools.partial(s_round, r == n_rounds - 1) for r in range(n_rounds)]
        return [s_nmat, s_intra, s_square] + rounds + [s_uw]

    states = [state_ref[h] for h in range(hh)]
    ws = {}

    def recur_stages(chunks):
        def r_first(c):
            for h in range(hh):
                lhs = jnp.concatenate([w_b[h, c], qg_b[h, c]], axis=0)
                ws[h] = _dot(lhs, states[h].astype(BF16))

        def r_second(c):
            for h in range(hh):
                v_new = (u[h, c] - ws[h][:CHUNK]).astype(BF16)
                lhs = jnp.concatenate([a_intra[h, c], kgt_b[h, c]], axis=0)
                both = _dot(lhs, v_new)
                o_c = ws[h][CHUNK:] + both[:CHUNK]
                states[h] = states[h] * eg_last[h, c] + both[CHUNK:]
                o_n = o_c * lax.rsqrt(jnp.mean(o_c * o_c, axis=-1, keepdims=True) + RMS_EPS) * nw
                o_ref[c * CHUNK:(c + 1) * CHUNK, h * hd:(h + 1) * hd] = o_n

        return [functools.partial(f, c) for c in chunks for f in (r_first, r_second)]

    gsz = min(DN_GROUP_CHUNKS, nck)
    groups = [list(range(g0, g0 + gsz)) for g0 in range(0, nck, gsz)]
    pending = []
    for chunks in groups:
        local = local_stages([(h, c) for c in chunks for h in range(hh)])
        for k, stage in enumerate(local):
            stage()
            if pending:
                pending.pop(0)()
        while pending:
            pending.pop(0)()
        pending = recur_stages(chunks)
    while pending:
        pending.pop(0)()
    for h in range(hh):
        state_ref[h] = states[h]


def _deltanet(dn3, ba3, bat, norm_w, tb):
    b, s, _ = dn3.shape
    hh = DN_HEADS
    nblk = s // tb
    return pl.pallas_call(
        _dn_kernel,
        grid=(b, s // tb),
        in_specs=[
            pl.BlockSpec((None, tb, DN_COLS), lambda bi, i: (bi, i, 0)),
            pl.BlockSpec((None, tb, LANES), lambda bi, i: (bi, i, 0)),
            pl.BlockSpec((BAT_ROWS, tb), lambda bi, i: (0, bi * nblk + i)),
            pl.BlockSpec((1, LANES), lambda bi, i: (0, 0)),
        ],
        out_specs=pl.BlockSpec((None, tb, DN_WIDTH), lambda bi, i: (bi, i, 0)),
        out_shape=jax.ShapeDtypeStruct((b, s, DN_WIDTH), F32),
        scratch_shapes=[pltpu.VMEM((hh, DN_HEAD_DIM, DN_HEAD_DIM), F32)],
        compiler_params=_cparams(("parallel", "arbitrary")),
        name="deltanet",
    )(dn3, ba3, bat, norm_w)


N_BIAS_LANES = 3
DA_UNROLL = 2


def _da_kernel(lq1_ref, lk1_ref, lq2_ref, lk2_ref, slope_ref, q_ref, k_ref, vt_ref, w_ref, o_ref,
                ka1_ref, ka2_ref, *scr):
    i = pl.program_id(2)
    tq = q_ref.shape[0]
    s_refs = ((scr[0], scr[1]), (scr[2], scr[3]))
    p_refs = ((scr[4], scr[5]), (scr[6], scr[7]))
    acc_refs = (scr[8], scr[9])
    st_ref = scr[10]
    kn_ref = scr[11]
    nkt, tk, _ = ka1_ref.shape
    half = DA_HEAD_DIM

    @pl.when(i == 0)
    def _():
        c = slope_ref[...] * LOG2E
        lane = lax.broadcasted_iota(jnp.int32, (tk, LANES), 1)
        row = lax.broadcasted_iota(jnp.int32, (tk, LANES), 0)

        def build(t, carry):
            kb = k_ref[pl.ds(pl.multiple_of(t * tk, tk), tk), :].astype(F32)
            bias = c * (t * tk + row).astype(F32)
            hi = bias.astype(BF16).astype(F32)
            rem = bias - hi
            mid = rem.astype(BF16).astype(F32)
            lo = rem - mid

            def feats(l0):
                return jnp.where(lane == l0, hi, jnp.where(lane == l0 + 1, mid,
                                                           jnp.where(lane == l0 + 2, lo, 0.0)))

            ka1_ref[t] = jnp.where(lane < half, kb, feats(half)).astype(BF16)
            ka2_ref[t] = jnp.where(lane >= half, kb, feats(0)).astype(BF16)
            sq = kb * kb
            n1 = jnp.max(jnp.sum(jnp.where(lane < half, sq, 0.0), axis=-1, keepdims=True), axis=0, keepdims=True)
            n2 = jnp.max(jnp.sum(jnp.where(lane >= half, sq, 0.0), axis=-1, keepdims=True), axis=0, keepdims=True)
            kn1, kn2 = carry
            return jnp.where(tile_lane == t, n1, kn1), jnp.where(tile_lane == t, n2, kn2)

        tile_lane = lax.broadcasted_iota(jnp.int32, (1, LANES), 1)
        zeros = jnp.zeros((1, LANES), F32)
        kn1, kn2 = lax.fori_loop(0, nkt, build, (zeros, zeros))
        kn_ref[0:1, :] = kn1
        kn_ref[1:2, :] = kn2

    qv = q_ref[...].astype(F32)
    lane_q = lax.broadcasted_iota(jnp.int32, qv.shape, 1)
    q1 = jnp.where(lane_q < half, qv, jnp.where(lane_q < half + N_BIAS_LANES, 1.0, 0.0))
    q2 = jnp.where(lane_q >= half, qv, jnp.where(lane_q < N_BIAS_LANES, 1.0, 0.0))
    qts = (q1.T.astype(BF16), q2.T.astype(BF16))
    kas = (ka1_ref, ka2_ref)

    SKIP_BELOW = -170.0
    NORM_SLACK = 1.01
    qsq = qv * qv
    qn = [jnp.sqrt(jnp.max(jnp.sum(jnp.where(msk, qsq, 0.0), axis=-1, keepdims=True), axis=0, keepdims=True))
          for msk in (lane_q < half, lane_q >= half)]
    tile_lane = lax.broadcasted_iota(jnp.int32, (1, LANES), 1)
    gap = ((i - tile_lane - 1) * tk + 1).astype(F32)
    c_row = slope_ref[...] * LOG2E
    bound = None
    for mp in range(2):
        kn = jnp.sqrt(kn_ref[mp:mp + 1, :])
        kd = jnp.max(jnp.where(tile_lane == i, kn, 0.0), axis=-1, keepdims=True)
        b_mp = qn[mp] * (kn + kd) * NORM_SLACK - c_row * gap
        bound = b_mp if bound is None else jnp.maximum(bound, b_mp)
    skippable = (bound < SKIP_BELOW) & (tile_lane < i)
    first_keep = jnp.min(jnp.where(skippable, float(LANES), tile_lane.astype(F32)), axis=-1, keepdims=True)
    base = first_keep.astype(jnp.int32)[0, 0]
    cnt = i - base

    M_ROW, A_ROW = 0, 2
    CH = 128
    SUB = 8
    V_AFTER_CHUNK = DA_TK // CH - 1

    def row(mp, r):
        return st_ref[mp, r:r + 1, :]

    def fold(x, op):
        return op(x.reshape(CH // SUB, SUB, tq), axis=0)

    def step(t, slot, s_mode, do_v=True, do_e=True):
        o = 1 - slot
        a_prevs = [row(mp, A_ROW + o) for mp in range(2)] if do_v else None
        for mp in range(2):
            m_t = row(mp, M_ROW + slot)
            tmax = jnp.full((SUB, tq), NEG_INF, F32)
            for c in range(tk // CH):
                rows = slice(c * CH, (c + 1) * CH)
                if do_e:
                    pc = jnp.exp2(s_refs[slot][mp][rows, :] - m_t)
                    p_refs[slot][mp][rows, :] = pc.astype(BF16)
                if s_mode is not None:
                    sc = _dot(kas[mp][base + t + 1, rows, :], qts[mp])
                    if s_mode != "plain":
                        krow = lax.broadcasted_iota(jnp.int32, (CH, tq), 0) + (c * CH + s_mode)
                        qcol = lax.broadcasted_iota(jnp.int32, (CH, tq), 1)
                        sc = jnp.where(krow <= qcol, sc, NEG_INF)
                    s_refs[o][mp][rows, :] = sc
                    tmax = jnp.maximum(tmax, fold(sc, jnp.max))
                if do_v and c == V_AFTER_CHUNK:
                    vt = vt_ref[base + t - 1]
                    acc_refs[mp][...] = a_prevs[mp] * acc_refs[mp][...] + _dot(vt, p_refs[o][mp][...])
            if s_mode is not None:
                m_new = jnp.maximum(m_t, jnp.max(tmax, axis=0, keepdims=True))
                st_ref[mp, M_ROW + o:M_ROW + o + 1, :] = m_new
                st_ref[mp, A_ROW + o:A_ROW + o + 1, :] = jnp.exp2(m_t - m_new)

    for mp in range(2):
        acc_refs[mp][...] = jnp.zeros((DA_V_AUG, tq), F32)
        st_ref[mp, M_ROW + 1:M_ROW + 2, :] = jnp.full((1, tq), -1e30, F32)

    @pl.when(cnt == 0)
    def _():
        step(-1, 1, 0, do_v=False, do_e=False)
        step(0, 0, None, do_v=False)
        step(1, 1, None, do_e=False)

    @pl.when(cnt == 1)
    def _():
        step(-1, 1, "plain", do_v=False, do_e=False)
        step(0, 0, 0, do_v=False)
        step(1, 1, None)
        step(2, 0, None, do_e=False)

    @pl.when(cnt >= 2)
    def _():
        step(-1, 1, "plain", do_v=False, do_e=False)
        step(0, 0, "plain", do_v=False)

    unroll = DA_UNROLL

    def body(it, carry):
        for k in range(unroll):
            step(1 + unroll * it + k, (1 + k) & 1, "plain")
        return carry

    n_plain = jnp.maximum(cnt - 2, 0)
    lax.fori_loop(0, n_plain // unroll, body, 0)

    for rem in range(unroll):
        @pl.when((cnt >= 2) & (n_plain % unroll == rem))
        def _(rem=rem):
            for k in range(rem):
                step(cnt - 1 - rem + k, (1 + k) & 1, "plain")
            step(cnt - 1, (1 + rem) & 1, 0)
            step(cnt, rem & 1, None)
            step(cnt + 1, (1 + rem) & 1, None, do_e=False)

    l1 = acc_refs[0][DA_V_DIM:DA_V_DIM + 1, :]
    l2 = acc_refs[1][DA_V_DIM:DA_V_DIM + 1, :]
    a1 = acc_refs[0][0:DA_V_DIM, :]
    a2 = acc_refs[1][0:DA_V_DIM, :]

    lam = (jnp.exp(jnp.sum(lq1_ref[...] * lk1_ref[...], axis=-1, keepdims=True))
           - jnp.exp(jnp.sum(lq2_ref[...] * lk2_ref[...], axis=-1, keepdims=True)) + LAM_INIT)
    o_t = a1 / l1 - lam * (a2 / l2)
    o_t = o_t * lax.rsqrt(jnp.mean(o_t * o_t, axis=0, keepdims=True) + RMS_EPS) * (1.0 - LAM_INIT)
    o_ref[...] = (o_t.T * w_ref[...]).astype(BF16)


def _diffattn(qk3, vt3, lq1, lk1, lq2, lk2, slopes, subln_w):
    b, s, _ = qk3.shape
    hh = DA_HEADS
    tq, tk = DA_TQ, DA_TK
    assert tq == tk and s % tq == 0
    nkt = s // tk
    c2 = lambda bi, h, i: (0, 0)
    return pl.pallas_call(
        _da_kernel,
        grid=(b, hh, s // tq),
        in_specs=[
            pl.BlockSpec((1, DA_HEAD_DIM), c2),
            pl.BlockSpec((1, DA_HEAD_DIM), c2),
            pl.BlockSpec((1, DA_HEAD_DIM), c2),
            pl.BlockSpec((1, DA_HEAD_DIM), c2),
            pl.BlockSpec((None, 1, LANES), lambda bi, h, i: (h, 0, 0)),
            pl.BlockSpec((None, tq, LANES), lambda bi, h, i: (bi, i, h)),
            pl.BlockSpec((None, s, LANES), lambda bi, h, i: (bi, 0, hh + h)),
            pl.BlockSpec((nkt, DA_V_AUG, tk), lambda bi, h, i: (bi, h, 0)),
            pl.BlockSpec((1, DA_V_DIM), c2),
        ],
        out_specs=pl.BlockSpec((None, tq, LANES), lambda bi, h, i: (bi, i, h)),
        out_shape=jax.ShapeDtypeStruct((b, s, DA_WIDTH), BF16),
        scratch_shapes=[
            pltpu.VMEM((nkt, tk, LANES), BF16),
            pltpu.VMEM((nkt, tk, LANES), BF16),
        ] + [pltpu.VMEM((tk, tq), F32)] * 4 + [pltpu.VMEM((tk, tq), BF16)] * 4 + [
            pltpu.VMEM((DA_V_AUG, tq), F32),
            pltpu.VMEM((DA_V_AUG, tq), F32),
            pltpu.VMEM((2, 8, tq), F32),
            pltpu.VMEM((8, LANES), F32),
        ],
        compiler_params=_cparams(("parallel", "parallel", "arbitrary")),
        name="diffattn",
    )(lq1, lk1, lq2, lk2, slopes, qk3, qk3, vt3, subln_w)


ROUTER_GROUP_LANE0 = N_EXPERTS
MIXER_ROW_PARTS = 2


def _mixer_kernel(x_ref, odn_ref, oda_ref, p_ref, eg_ref, eb_ref, wz_ref, bz_ref, wg_ref, bg_ref,
                  wdn_ref, wda_ref, wout_ref, g1_ref, b1_ref, wr_cat_ref, wr_hi_ref, br_ref,
                  wpg_ref, bpg_ref, wpp_ref, h1b_ref, base_ref, logits_ref):
    n_parts = MIXER_ROW_PARTS
    pm = x_ref.shape[0] // n_parts
    parts = [slice(r * pm, (r + 1) * pm) for r in range(n_parts)]
    st = [dict() for _ in parts]
    for s, rows in zip(st, parts):
        s["h"] = _layer_norm(x_ref[rows, :], eg_ref[...], eb_ref[...])
        s["hb"] = s["h"].astype(BF16)
    for s, rows in zip(st, parts):
        s["z"] = _dot(s["hb"], wz_ref[...]) + bz_ref[...]
        s["gl"] = _dot(s["hb"], wg_ref[...]) + bg_ref[...]
    for s, rows in zip(st, parts):
        s["y_dn"] = _dot((odn_ref[rows, :] * _silu(s["z"])).astype(BF16), wdn_ref[...])
        s["y_da"] = _dot(oda_ref[rows, :], wda_ref[...])
    for s, rows in zip(st, parts):
        gates = _sigmoid(s["gl"])
        mixed = gates[:, :D_MODEL] * s["y_dn"] + gates[:, D_MODEL:] * s["y_da"]
        s["mix"] = _dot(mixed.astype(BF16), wout_ref[...])
    for s, rows in zip(st, parts):
        s["h1"] = _layer_norm(ALPHA * s["h"] + s["mix"], g1_ref[...], b1_ref[...])
        s["h1b"] = s["h1"].astype(BF16)
        h1b_ref[rows, :] = s["h1b"]
    for s, rows in zip(st, parts):
        ple = (_sigmoid(_dot(s["h1b"], wpg_ref[...]) + bpg_ref[...])
               * _dot(p_ref[rows, :].astype(BF16), wpp_ref[...]))
        base_ref[rows, :] = ALPHA * s["h1"] + ple
        h1_lo = (s["h1"] - s["h1b"].astype(F32)).astype(BF16)
        both = _dot(s["h1b"], wr_cat_ref[...])
        s["logits"] = (both[:, :LANES] + both[:, LANES:] + _dot(h1_lo, wr_hi_ref[...]) + br_ref[...])
    for s, rows in zip(st, parts):
        logits_ref[rows, :] = s["logits"]


def _route(logits):
    tm = logits.shape[0]
    lane = lax.broadcasted_iota(jnp.int32, (tm, LANES), 1).astype(F32)
    big = float(LANES)
    is_g = (lane >= ROUTER_GROUP_LANE0) & (lane < ROUTER_GROUP_LANE0 + N_GROUPS)
    gl = jnp.where(is_g, logits, NEG_INF)
    gmax = jnp.max(gl, axis=-1, keepdims=True)
    gsum = jnp.sum(jnp.exp(gl - gmax), axis=-1, keepdims=True)
    g_val = 1.0 / gsum
    g_idx = jnp.min(jnp.where(gl == gmax, lane, big), axis=-1, keepdims=True) - ROUTER_GROUP_LANE0
    e_lo = g_idx * EXPERTS_PER_GROUP
    in_g = (lane >= e_lo) & (lane < e_lo + EXPERTS_PER_GROUP)
    el = jnp.where(in_g, logits, NEG_INF)
    emax = jnp.max(el, axis=-1, keepdims=True)
    ee = jnp.exp(el - emax)
    e_prob = ee / jnp.sum(ee, axis=-1, keepdims=True)
    p1 = jnp.max(e_prob, axis=-1, keepdims=True)
    i1 = jnp.min(jnp.where(in_g & (e_prob == p1), lane, big), axis=-1, keepdims=True)
    rest = jnp.where(in_g & (lane != i1), e_prob, -1.0)
    p2 = jnp.max(rest, axis=-1, keepdims=True)
    i2 = jnp.min(jnp.where(rest == p2, lane, big), axis=-1, keepdims=True)
    denom = p1 + p2
    return (jnp.where(lane == i1, p1 / denom * g_val, 0.0)
            + jnp.where(lane == i2, p2 / denom * g_val, 0.0))


def _mixer(x2, odn2, oda2, p2, ws, tm):
    n = x2.shape[0]
    row = lambda i: (i, 0)
    const = lambda i: (0, 0)

    def cspec(a):
        return pl.BlockSpec(a.shape, const, pipeline_mode=pl.Buffered(1))

    (eg, eb, wz, bz, wg, bg, wdn, wda, wout, g1, b1, wr_cat, wr_hi, br, wpg, bpg, wpp) = ws
    return pl.pallas_call(
        _mixer_kernel,
        grid=(n // tm,),
        in_specs=[
            pl.BlockSpec((tm, D_MODEL), row),
            pl.BlockSpec((tm, DN_WIDTH), row),
            pl.BlockSpec((tm, DA_WIDTH), row),
            pl.BlockSpec((tm, PLE_DIM), row),
        ] + [cspec(a) for a in ws],
        out_specs=[
            pl.BlockSpec((tm, D_MODEL), row),
            pl.BlockSpec((tm, D_MODEL), row),
            pl.BlockSpec((tm, LANES), row),
        ],
        out_shape=[
            jax.ShapeDtypeStruct((n, D_MODEL), BF16),
            jax.ShapeDtypeStruct((n, D_MODEL), F32),
            jax.ShapeDtypeStruct((n, LANES), F32),
        ],
        compiler_params=_cparams(("parallel",)),
        name="mixer",
    )(x2, odn2, oda2, p2, *ws)


def _moe_kernel(h_ref, base_ref, logits_ref, wg_ref, wu_ref, wd_ref, g2_ref, b2_ref, o_ref, acc_ref):
    g = pl.program_id(1)

    @pl.when(g == 0)
    def _():
        acc_ref[...] = base_ref[...]

    comb = _route(logits_ref[...])
    lane = lax.broadcasted_iota(jnp.int32, comb.shape, 1)
    h = h_ref[...]
    cols = []
    for e in range(EXPERTS_PER_GROUP):
        c = jnp.sum(jnp.where(lane == g * EXPERTS_PER_GROUP + e, comb, 0.0), axis=-1, keepdims=True)
        hid = _silu(_dot(h, wg_ref[e])) * _dot(h, wu_ref[e])
        cols.append((hid * c).astype(BF16))
    hid_all = jnp.concatenate(cols, axis=1)
    wd = wd_ref[...].reshape(EXPERTS_PER_GROUP * EXPERT_FF, D_MODEL)
    acc_ref[...] += _dot(hid_all, wd)

    @pl.when(g == N_GROUPS - 1)
    def _():
        o_ref[...] = _layer_norm(acc_ref[...], g2_ref[...], b2_ref[...])


def _moe(h1b, base, logits, wg, wu, wd, g2, b2, tm):
    n = h1b.shape[0]
    row = lambda i, g: (i, 0)
    return pl.pallas_call(
        _moe_kernel,
        grid=(n // tm, N_GROUPS),
        in_specs=[
            pl.BlockSpec((tm, D_MODEL), row),
            pl.BlockSpec((tm, D_MODEL), row),
            pl.BlockSpec((tm, LANES), row),
            pl.BlockSpec((EXPERTS_PER_GROUP, D_MODEL, EXPERT_FF), lambda i, g: (g, 0, 0)),
            pl.BlockSpec((EXPERTS_PER_GROUP, D_MODEL, EXPERT_FF), lambda i, g: (g, 0, 0)),
            pl.BlockSpec((EXPERTS_PER_GROUP, EXPERT_FF, D_MODEL), lambda i, g: (g, 0, 0)),
            pl.BlockSpec((1, D_MODEL), lambda i, g: (0, 0)),
            pl.BlockSpec((1, D_MODEL), lambda i, g: (0, 0)),
        ],
        out_specs=pl.BlockSpec((tm, D_MODEL), row),
        out_shape=jax.ShapeDtypeStruct((n, D_MODEL), F32),
        scratch_shapes=[pltpu.VMEM((tm, D_MODEL), F32)],
        compiler_params=_cparams(("parallel", "arbitrary")),
        name="moe",
    )(h1b, base, logits, wg, wu, wd, g2, b2)


def _row(v):
    return v.reshape(1, -1).astype(F32)


def _pad_lanes(v, offset=0):
    out = jnp.zeros((1, LANES), F32)
    return lax.dynamic_update_slice(out, v.reshape(1, -1).astype(F32), (0, offset))


def _tile(n, pref):
    t = min(pref, n)
    while n % t:
        t //= 2
    return t


def kernel(x, p, emb_ln_g, emb_ln_b, w_in, b_in, conv_w, dn_a_log, dn_dt_bias, dn_norm_w, w_dn_o,
           da_lq1, da_lk1, da_lq2, da_lk2, da_subln_w, w_da_o, w_out, ln1_g, ln1_b,
           w_router_group, b_router_group, w_router_expert, b_router_expert,
           w_exp_gate, w_exp_up, w_exp_down, w_ple_gate, b_ple_gate, w_ple_proj, ln2_g, ln2_b):
    B, S, _ = x.shape
    n = B * S
    assert S % CHUNK == 0
    li = 0
    offs = np.concatenate([[0], np.cumsum(IN_SIZES)]).astype(int)
    c_dn, c_z, c_b, c_a, c_q, c_k, c_v, c_g = [slice(int(offs[j]), int(offs[j + 1])) for j in range(8)]
    w_i, b_i = w_in[li], b_in[li]

    small_w = jnp.zeros((D_MODEL, LANES), F32)
    small_w = small_w.at[:, 0:DN_HEADS].set(w_i[:, c_b]).at[:, DN_HEADS:2 * DN_HEADS].set(w_i[:, c_a])
    small_b = jnp.zeros((LANES,), F32)
    small_b = small_b.at[0:DN_HEADS].set(b_i[c_b]).at[DN_HEADS:2 * DN_HEADS].set(b_i[c_a])
    w_a = jnp.concatenate([w_i[:, c_dn], small_w, w_i[:, c_q], w_i[:, c_k], w_i[:, c_v]], axis=1).astype(BF16)
    b_a = jnp.concatenate([b_i[c_dn], small_b, b_i[c_q], b_i[c_k], b_i[c_v]]).reshape(1, -1).astype(F32)
    alog = _pad_lanes(dn_a_log[li], DN_HEADS)
    dtb = _pad_lanes(dn_dt_bias[li], DN_HEADS)

    x2 = x.reshape(n, D_MODEL)
    tm_a = _tile(n, 512)
    dn_raw, ba, bat, qk, vt = _in_proj(x2, _row(emb_ln_g), _row(emb_ln_b), w_a, b_a, alog, dtb,
                                  conv_w[li].astype(F32), tm_a, S)

    tb = _tile(S, 1024)
    o_dn = _deltanet(dn_raw.reshape(B, S, DN_COLS), ba.reshape(B, S, LANES), bat, _row(dn_norm_w[li]), tb)

    slopes = jnp.asarray(2.0 ** (-8.0 * np.arange(1, DA_HEADS + 1) / DA_HEADS), dtype=F32)
    slopes = jnp.broadcast_to(slopes[:, None, None], (DA_HEADS, 1, LANES))
    o_da = _diffattn(qk.reshape(B, S, 2 * DA_QK_WIDTH), vt, _row(da_lq1[li]), _row(da_lk1[li]),
                      _row(da_lq2[li]), _row(da_lk2[li]), slopes, _row(da_subln_w[li]))

    wr = jnp.zeros((D_MODEL, LANES), F32)
    wr = wr.at[:, 0:N_EXPERTS].set(w_router_expert[li])
    wr = wr.at[:, ROUTER_GROUP_LANE0:ROUTER_GROUP_LANE0 + N_GROUPS].set(w_router_group[li])
    wr_hi = wr.astype(BF16)
    wr_lo = (wr - wr_hi.astype(F32)).astype(BF16)
    br = jnp.zeros((LANES,), F32).at[0:N_EXPERTS].set(b_router_expert[li])
    br = br.at[ROUTER_GROUP_LANE0:ROUTER_GROUP_LANE0 + N_GROUPS].set(b_router_group[li]).reshape(1, LANES)
    ws = (_row(emb_ln_g), _row(emb_ln_b),
          w_i[:, c_z].astype(BF16), _row(b_i[c_z]),
          w_i[:, c_g].astype(BF16), _row(b_i[c_g]),
          w_dn_o[li].astype(BF16), w_da_o[li].astype(BF16), w_out[li].astype(BF16),
          _row(ln1_g[li]), _row(ln1_b[li]), jnp.concatenate([wr_hi, wr_lo], axis=1), wr_hi, br,
          w_ple_gate[li].astype(BF16), _row(b_ple_gate[li]), w_ple_proj[li].astype(BF16))
    tm_c = _tile(n, 512)
    h1b, base, logits = _mixer(x2, o_dn.reshape(n, DN_WIDTH), o_da.reshape(n, DA_WIDTH),
                             p[li].reshape(n, PLE_DIM), ws, tm_c)

    wg = w_exp_gate[li].astype(BF16)
    wu = w_exp_up[li].astype(BF16)
    wd = w_exp_down[li].astype(BF16)
    tm_d = _tile(n, 1024)
    out = _moe(h1b, base, logits, wg, wu, wd, _row(ln2_g[li]), _row(ln2_b[li]), tm_d)
    return out.reshape(B, S, D_MODEL)
```

```python
import functools
import math

import jax
import jax.numpy as jnp
import numpy as np
from jax import lax
from jax.experimental import pallas as pl
from jax.experimental.pallas import tpu as pltpu

F32 = jnp.float32
BF16 = jnp.bfloat16

D_MODEL = 1024
PLE_DIM = 256
DN_HEADS = 4
DN_HEAD_DIM = 128
DN_WIDTH = DN_HEADS * DN_HEAD_DIM
CONV_WIDTH = 4
CHUNK = 64
DA_HEADS = 4
DA_HEAD_DIM = 64
DA_V_DIM = 2 * DA_HEAD_DIM
DA_QK_WIDTH = DA_HEADS * 2 * DA_HEAD_DIM
DA_WIDTH = DA_HEADS * DA_V_DIM
N_GROUPS = 4
EXPERTS_PER_GROUP = 4
N_EXPERTS = N_GROUPS * EXPERTS_PER_GROUP
EXPERT_FF = 256
IN_SIZES = (3 * DN_WIDTH, DN_WIDTH, DN_HEADS, DN_HEADS, DA_QK_WIDTH, DA_QK_WIDTH, DA_WIDTH, 2 * D_MODEL)
LN_EPS = 1e-5
RMS_EPS = 1e-6
DEPTH = 1
ALPHA = (2.0 * DEPTH) ** 0.25
LAM_INIT = 0.8 - 0.6 * math.exp(-0.3 * 0)

LANES = 128
VMEM_LIMIT = 56 * 1024 * 1024

NEG_INF = float("-inf")


def _cparams(sem, flags=None):
    return pltpu.CompilerParams(dimension_semantics=sem, vmem_limit_bytes=VMEM_LIMIT, flags=flags)


def _layer_norm(x, g, b):
    mu = jnp.mean(x, axis=-1, keepdims=True)
    xc = x - mu
    var = jnp.mean(xc * xc, axis=-1, keepdims=True)
    return xc * lax.rsqrt(var + LN_EPS) * g + b


def _sigmoid(x):
    return 1.0 / (1.0 + jnp.exp2(x * (-LOG2E)))


def _silu(x):
    return x * _sigmoid(x)


def _dot(a, b):
    return jnp.dot(a, b, preferred_element_type=F32)


def _dot_nt(a, b):
    return lax.dot_general(a, b, (((1,), (1,)), ((), ())), preferred_element_type=F32)


DN_COLS = 3 * DN_WIDTH
DA_V_PAD = 16
DA_V_AUG = DA_V_DIM + DA_V_PAD
DA_TK = 512
DA_TQ = DA_TK
LOG2E = 1.4426950408889634


IN_PROJ_ROW_PARTS = 2
TAIL = 8
BAT_ROWS = 8


def _in_proj_kernel(tiles_per_seq, x_ref, g_ref, b_ref, w_ref, bias_ref, alog_ref, dtb_ref, cw_ref,
                    dn_ref, ba_ref, bat_ref, qk_ref, vt_ref, xe_ref):
    tm = x_ref.shape[0]
    assert vt_ref.shape[0] == 1 and tm == DA_TK
    n_parts = IN_PROJ_ROW_PARTS
    pm = tm // n_parts
    parts = [slice(r * pm, (r + 1) * pm) for r in range(n_parts)]
    c0 = DN_COLS + LANES
    qw = DA_QK_WIDTH

    def proj(hb, lo, hi):
        return _dot(hb, w_ref[:, lo:hi]) + bias_ref[:, lo:hi]

    @pl.when(pl.program_id(0) % tiles_per_seq == 0)
    def _():
        xe_ref[tm:tm + TAIL, :] = jnp.zeros((TAIL, DN_COLS), F32)

    xe_ref[0:TAIL, :] = xe_ref[tm:tm + TAIL, :]
    hbs = [_layer_norm(x_ref[rows, :], g_ref[...], b_ref[...]).astype(BF16) for rows in parts]
    for hb, rows in zip(hbs, parts):
        xe_ref[TAIL + rows.start:TAIL + rows.stop, :] = proj(hb, 0, DN_COLS)
    for hb, rows in zip(hbs, parts):
        qk_ref[rows, :qw] = (proj(hb, c0, c0 + qw) * (DA_HEAD_DIM ** -0.5 * LOG2E)).astype(BF16)
        qk_ref[rows, qw:] = proj(hb, c0 + qw, c0 + 2 * qw).astype(BF16)
    for hb, rows in zip(hbs, parts):
        v_t = proj(hb, c0 + 2 * qw, c0 + 3 * qw).T.astype(BF16)
        extra = (lax.broadcasted_iota(jnp.int32, (DA_V_PAD, pm), 0) == 0).astype(F32).astype(BF16)
        vt_ref[0, :, rows] = jnp.concatenate(
            [blk for hd in range(DA_HEADS) for blk in (v_t[hd * DA_V_DIM:(hd + 1) * DA_V_DIM], extra)], axis=0)
    for hb, rows in zip(hbs, parts):
        ba = proj(hb, DN_COLS, DN_COLS + LANES)
        lane = lax.broadcasted_iota(jnp.int32, (pm, LANES), 1)
        row = lax.broadcasted_iota(jnp.int32, (pm, LANES), 0)
        beta = _sigmoid(ba)
        xs = ba + dtb_ref[...]
        softplus = jnp.maximum(xs, 0.0) + jnp.log(1.0 + jnp.exp(-jnp.abs(xs)))
        g = -jnp.exp(alog_ref[...]) * softplus
        g = jnp.where((lane >= DN_HEADS) & (lane < 2 * DN_HEADS), g, 0.0)
        rin = row & (CHUNK - 1)
        k = 1
        while k < CHUNK:
            g = g + jnp.where(rin >= k, pltpu.roll(g, k, 0), 0.0)
            k *= 2
        ba_out = jnp.where(lane < DN_HEADS, beta, g)
        ba_ref[rows, :] = ba_out
        bat_ref[:, rows] = ba_out.T[0:BAT_ROWS, :]

    cw = cw_ref[...]
    hd = DN_HEAD_DIM
    for rows in parts:
        for hh in range(3 * DN_HEADS):
            cols = slice(hh * hd, (hh + 1) * hd)
            r0 = TAIL + rows.start
            xh = xe_ref[r0:r0 + pm, cols] * cw[CONV_WIDTH - 1:CONV_WIDTH, cols]
            for j in range(CONV_WIDTH - 1):
                off = r0 - (CONV_WIDTH - 1) + j
                xh = xh + xe_ref[off:off + pm, cols] * cw[j:j + 1, cols]
            xh = _silu(xh)
            if hh < 2 * DN_HEADS:
                scale = (hd ** -0.5) if hh < DN_HEADS else 1.0
                xh = xh * lax.rsqrt(jnp.sum(xh * xh, axis=-1, keepdims=True) + RMS_EPS) * scale
            dn_ref[rows, cols] = xh


def _in_proj(x2, emb_g, emb_b, w_a, b_a, alog, dtb, conv_w, tm, seq):
    n = x2.shape[0]
    wcols = w_a.shape[1]
    assert seq % tm == 0
    const = lambda i: (0, 0)
    return pl.pallas_call(
        functools.partial(_in_proj_kernel, seq // tm),
        grid=(n // tm,),
        in_specs=[
            pl.BlockSpec((tm, D_MODEL), lambda i: (i, 0)),
            pl.BlockSpec((1, D_MODEL), const),
            pl.BlockSpec((1, D_MODEL), const),
            pl.BlockSpec((D_MODEL, wcols), const),
            pl.BlockSpec((1, wcols), const),
            pl.BlockSpec((1, LANES), const),
            pl.BlockSpec((1, LANES), const),
            pl.BlockSpec((CONV_WIDTH, DN_COLS), const),
        ],
        out_specs=[
            pl.BlockSpec((tm, DN_COLS), lambda i: (i, 0)),
            pl.BlockSpec((tm, LANES), lambda i: (i, 0)),
            pl.BlockSpec((BAT_ROWS, tm), lambda i: (0, i)),
            pl.BlockSpec((tm, 2 * DA_QK_WIDTH), lambda i: (i, 0)),
            pl.BlockSpec((tm // DA_TK, DA_HEADS * DA_V_AUG, DA_TK), lambda i: (i, 0, 0)),
        ],
        out_shape=[
            jax.ShapeDtypeStruct((n, DN_COLS), F32),
            jax.ShapeDtypeStruct((n, LANES), F32),
            jax.ShapeDtypeStruct((BAT_ROWS, n), F32),
            jax.ShapeDtypeStruct((n, 2 * DA_QK_WIDTH), BF16),
            jax.ShapeDtypeStruct((n // DA_TK, DA_HEADS * DA_V_AUG, DA_TK), BF16),
        ],
        scratch_shapes=[pltpu.VMEM((tm + 2 * TAIL, DN_COLS), F32)],
        compiler_params=_cparams(("arbitrary",)),
        name="in_proj",
    )(x2, emb_g, emb_b, w_a, b_a, alog, dtb, conv_w)


DN_GROUP_CHUNKS = 4


def _dn_kernel(x_ref, ba_ref, gt_ref, nw_ref, o_ref, state_ref):
    sblk = pl.program_id(1)
    tb = x_ref.shape[0]
    nck = tb // CHUNK
    hh = DN_HEADS
    hd = DN_HEAD_DIM

    @pl.when(sblk == 0)
    def _():
        state_ref[...] = jnp.zeros_like(state_ref)

    ba = ba_ref[...]
    ri = lax.broadcasted_iota(jnp.int32, (CHUNK, CHUNK), 0)
    ci = lax.broadcasted_iota(jnp.int32, (CHUNK, CHUNK), 1)
    causal = ri >= ci
    strict = ri > ci
    eye = jnp.where(ri == ci, 1.0, 0.0).astype(F32)
    nw = nw_ref[...]

    q_b, k_b, kb_b, rhs_uw, qg_b, kgt_b, decay, eg_last = {}, {}, {}, {}, {}, {}, {}, {}
    for h in range(hh):
        q = x_ref[:, h * hd:(h + 1) * hd]
        k = x_ref[:, (hh + h) * hd:(hh + h + 1) * hd]
        v = x_ref[:, (2 * hh + h) * hd:(2 * hh + h + 1) * hd]
        bcol = ba[:, h:h + 1]
        gcol = ba[:, hh + h:hh + h + 1]
        grow = gt_ref[hh + h:hh + h + 1, :]
        egcol = jnp.exp(gcol)
        kb = k * bcol
        uw = jnp.concatenate([v * bcol, kb * egcol], axis=1).astype(BF16)
        qg = (q * egcol).astype(BF16)
        qb = q.astype(BF16)
        kbb = kb.astype(BF16)
        kk = k.astype(BF16)
        for c in range(nck):
            sl = slice(c * CHUNK, (c + 1) * CHUNK)
            gc = gcol[sl]
            g_last = gc[CHUNK - 1:CHUNK, :]
            q_b[h, c], k_b[h, c], kb_b[h, c] = qb[sl], kk[sl], kbb[sl]
            rhs_uw[h, c], qg_b[h, c] = uw[sl], qg[sl]
            kgt_b[h, c] = (k[sl] * jnp.exp(g_last - gc)).T.astype(BF16)
            decay[h, c] = jnp.exp(jnp.where(causal, gc - grow[:, sl], NEG_INF))
            eg_last[h, c] = jnp.exp(g_last)

    power, t, a_intra, u, w_b = {}, {}, {}, {}, {}
    n_rounds = int(math.log2(CHUNK)) - 1

    def local_stages(group):
        def s_nmat():
            for p_ in group:
                nmat = jnp.where(strict, -_dot_nt(kb_b[p_], k_b[p_]) * decay[p_], 0.0)
                power[p_] = nmat
                t[p_] = eye + nmat

        def s_intra():
            for p_ in group:
                a_intra[p_] = (_dot_nt(q_b[p_], k_b[p_]) * decay[p_]).astype(BF16)

        def s_square():
            for p_ in group:
                pb = power[p_].astype(BF16)
                power[p_] = _dot(pb, pb)

        def s_round(last):
            for p_ in group:
                pb = power[p_].astype(BF16)
                if last:
                    t[p_] = t[p_] + _dot(t[p_].astype(BF16), pb)
                else:
                    both = _dot(jnp.concatenate([power[p_], t[p_]], axis=0).astype(BF16), pb)
                    power[p_] = both[:CHUNK]
                    t[p_] = t[p_] + both[CHUNK:]

        def s_uw():
            for p_ in group:
                uw = _dot(t[p_].astype(BF16), rhs_uw[p_])
                u[p_] = uw[:, :hd]
                w_b[p_] = uw[:, hd:].astype(BF16)

        rounds = [functools.partial(s_round, r == n_rounds - 1) for r in range(n_rounds)]
        return [s_nmat, s_intra, s_square] + rounds + [s_uw]

    states = [state_ref[h] for h in range(hh)]
    ws = {}

    def recur_stages(chunks):
        def r_first(c):
            for h in range(hh):
                lhs = jnp.concatenate([w_b[h, c], qg_b[h, c]], axis=0)
                ws[h] = _dot(lhs, states[h].astype(BF16))

        def r_second(c):
            for h in range(hh):
                v_new = (u[h, c] - ws[h][:CHUNK]).astype(BF16)
                lhs = jnp.concatenate([a_intra[h, c], kgt_b[h, c]], axis=0)
                both = _dot(lhs, v_new)
                o_c = ws[h][CHUNK:] + both[:CHUNK]
                states[h] = states[h] * eg_last[h, c] + both[CHUNK:]
                o_n = o_c * lax.rsqrt(jnp.mean(o_c * o_c, axis=-1, keepdims=True) + RMS_EPS) * nw
                o_ref[c * CHUNK:(c + 1) * CHUNK, h * hd:(h + 1) * hd] = o_n

        return [functools.partial(f, c) for c in chunks for f in (r_first, r_second)]

    gsz = min(DN_GROUP_CHUNKS, nck)
    groups = [list(range(g0, g0 + gsz)) for g0 in range(0, nck, gsz)]
    pending = []
    for chunks in groups:
        local = local_stages([(h, c) for c in chunks for h in range(hh)])
        for k, stage in enumerate(local):
            stage()
            if pending:
                pending.pop(0)()
        while pending:
            pending.pop(0)()
        pending = recur_stages(chunks)
    while pending:
        pending.pop(0)()
    for h in range(hh):
        state_ref[h] = states[h]


def _deltanet(dn3, ba3, bat, norm_w, tb):
    b, s, _ = dn3.shape
    hh = DN_HEADS
    nblk = s // tb
    return pl.pallas_call(
        _dn_kernel,
        grid=(b, s // tb),
        in_specs=[
            pl.BlockSpec((None, tb, DN_COLS), lambda bi, i: (bi, i, 0)),
            pl.BlockSpec((None, tb, LANES), lambda bi, i: (bi, i, 0)),
            pl.BlockSpec((BAT_ROWS, tb), lambda bi, i: (0, bi * nblk + i)),
            pl.BlockSpec((1, LANES), lambda bi, i: (0, 0)),
        ],
        out_specs=pl.BlockSpec((None, tb, DN_WIDTH), lambda bi, i: (bi, i, 0)),
        out_shape=jax.ShapeDtypeStruct((b, s, DN_WIDTH), F32),
        scratch_shapes=[pltpu.VMEM((hh, DN_HEAD_DIM, DN_HEAD_DIM), F32)],
        compiler_params=_cparams(("parallel", "arbitrary")),
        name="deltanet",
    )(dn3, ba3, bat, norm_w)


N_BIAS_LANES = 3
DA_UNROLL = 2


def _da_kernel(lq1_ref, lk1_ref, lq2_ref, lk2_ref, slope_ref, q_ref, k_ref, vt_ref, w_ref, o_ref,
                ka1_ref, ka2_ref, *scr):
    i = pl.program_id(2)
    tq = q_ref.shape[0]
    s_refs = ((scr[0], scr[1]), (scr[2], scr[3]))
    p_refs = ((scr[4], scr[5]), (scr[6], scr[7]))
    acc_refs = (scr[8], scr[9])
    st_ref = scr[10]
    kn_ref = scr[11]
    nkt, tk, _ = ka1_ref.shape
    half = DA_HEAD_DIM

    @pl.when(i == 0)
    def _():
        c = slope_ref[...] * LOG2E
        lane = lax.broadcasted_iota(jnp.int32, (tk, LANES), 1)
        row = lax.broadcasted_iota(jnp.int32, (tk, LANES), 0)

        def build(t, carry):
            kb = k_ref[pl.ds(pl.multiple_of(t * tk, tk), tk), :].astype(F32)
            bias = c * (t * tk + row).astype(F32)
            hi = bias.astype(BF16).astype(F32)
            rem = bias - hi
            mid = rem.astype(BF16).astype(F32)
            lo = rem - mid

            def feats(l0):
                return jnp.where(lane == l0, hi, jnp.where(lane == l0 + 1, mid,
                                                           jnp.where(lane == l0 + 2, lo, 0.0)))

            ka1_ref[t] = jnp.where(lane < half, kb, feats(half)).astype(BF16)
            ka2_ref[t] = jnp.where(lane >= half, kb, feats(0)).astype(BF16)
            sq = kb * kb
            n1 = jnp.max(jnp.sum(jnp.where(lane < half, sq, 0.0), axis=-1, keepdims=True), axis=0, keepdims=True)
            n2 = jnp.max(jnp.sum(jnp.where(lane >= half, sq, 0.0), axis=-1, keepdims=True), axis=0, keepdims=True)
            kn1, kn2 = carry
            return jnp.where(tile_lane == t, n1, kn1), jnp.where(tile_lane == t, n2, kn2)

        tile_lane = lax.broadcasted_iota(jnp.int32, (1, LANES), 1)
        zeros = jnp.zeros((1, LANES), F32)
        kn1, kn2 = lax.fori_loop(0, nkt, build, (zeros, zeros))
        kn_ref[0:1, :] = kn1
        kn_ref[1:2, :] = kn2

    qv = q_ref[...].astype(F32)
    lane_q = lax.broadcasted_iota(jnp.int32, qv.shape, 1)
    q1 = jnp.where(lane_q < half, qv, jnp.where(lane_q < half + N_BIAS_LANES, 1.0, 0.0))
    q2 = jnp.where(lane_q >= half, qv, jnp.where(lane_q < N_BIAS_LANES, 1.0, 0.0))
    qts = (q1.T.astype(BF16), q2.T.astype(BF16))
    kas = (ka1_ref, ka2_ref)

    SKIP_BELOW = -170.0
    NORM_SLACK = 1.01
    qsq = qv * qv
    qn = [jnp.sqrt(jnp.max(jnp.sum(jnp.where(msk, qsq, 0.0), axis=-1, keepdims=True), axis=0, keepdims=True))
          for msk in (lane_q < half, lane_q >= half)]
    tile_lane = lax.broadcasted_iota(jnp.int32, (1, LANES), 1)
    gap = ((i - tile_lane - 1) * tk + 1).astype(F32)
    c_row = slope_ref[...] * LOG2E
    bound = None
    for mp in range(2):
        kn = jnp.sqrt(kn_ref[mp:mp + 1, :])
        kd = jnp.max(jnp.where(tile_lane == i, kn, 0.0), axis=-1, keepdims=True)
        b_mp = qn[mp] * (kn + kd) * NORM_SLACK - c_row * gap
        bound = b_mp if bound is None else jnp.maximum(bound, b_mp)
    skippable = (bound < SKIP_BELOW) & (tile_lane < i)
    first_keep = jnp.min(jnp.where(skippable, float(LANES), tile_lane.astype(F32)), axis=-1, keepdims=True)
    base = first_keep.astype(jnp.int32)[0, 0]
    cnt = i - base

    M_ROW, A_ROW = 0, 2
    CH = 128
    SUB = 8
    V_AFTER_CHUNK = DA_TK // CH - 1

    def row(mp, r):
        return st_ref[mp, r:r + 1, :]

    def fold(x, op):
        return op(x.reshape(CH // SUB, SUB, tq), axis=0)

    def step(t, slot, s_mode, do_v=True, do_e=True):
        o = 1 - slot
        a_prevs = [row(mp, A_ROW + o) for mp in range(2)] if do_v else None
        for mp in range(2):
            m_t = row(mp, M_ROW + slot)
            tmax = jnp.full((SUB, tq), NEG_INF, F32)
            for c in range(tk // CH):
                rows = slice(c * CH, (c + 1) * CH)
                if do_e:
                    pc = jnp.exp2(s_refs[slot][mp][rows, :] - m_t)
                    p_refs[slot][mp][rows, :] = pc.astype(BF16)
                if s_mode is not None:
                    sc = _dot(kas[mp][base + t + 1, rows, :], qts[mp])
                    if s_mode != "plain":
                        krow = lax.broadcasted_iota(jnp.int32, (CH, tq), 0) + (c * CH + s_mode)
                        qcol = lax.broadcasted_iota(jnp.int32, (CH, tq), 1)
                        sc = jnp.where(krow <= qcol, sc, NEG_INF)
                    s_refs[o][mp][rows, :] = sc
                    tmax = jnp.maximum(tmax, fold(sc, jnp.max))
                if do_v and c == V_AFTER_CHUNK:
                    vt = vt_ref[base + t - 1]
                    acc_refs[mp][...] = a_prevs[mp] * acc_refs[mp][...] + _dot(vt, p_refs[o][mp][...])
            if s_mode is not None:
                m_new = jnp.maximum(m_t, jnp.max(tmax, axis=0, keepdims=True))
                st_ref[mp, M_ROW + o:M_ROW + o + 1, :] = m_new
                st_ref[mp, A_ROW + o:A_ROW + o + 1, :] = jnp.exp2(m_t - m_new)

    for mp in range(2):
        acc_refs[mp][...] = jnp.zeros((DA_V_AUG, tq), F32)
        st_ref[mp, M_ROW + 1:M_ROW + 2, :] = jnp.full((1, tq), -1e30, F32)

    def finish():
        l1 = acc_refs[0][DA_V_DIM:DA_V_DIM + 1, :]
        l2 = acc_refs[1][DA_V_DIM:DA_V_DIM + 1, :]
        a1 = acc_refs[0][0:DA_V_DIM, :]
        a2 = acc_refs[1][0:DA_V_DIM, :]
        lam = (jnp.exp(jnp.sum(lq1_ref[...] * lk1_ref[...], axis=-1, keepdims=True))
               - jnp.exp(jnp.sum(lq2_ref[...] * lk2_ref[...], axis=-1, keepdims=True)) + LAM_INIT)
        o_t = a1 / l1 - lam * (a2 / l2)
        o_t = o_t * lax.rsqrt(jnp.mean(o_t * o_t, axis=0, keepdims=True) + RMS_EPS) * (1.0 - LAM_INIT)
        o_ref[...] = (o_t.T * w_ref[...]).astype(BF16)

    @pl.when(cnt == 0)
    def _():
        step(-1, 1, 0, do_v=False, do_e=False)
        step(0, 0, None, do_v=False)
        step(1, 1, None, do_e=False)
        finish()

    @pl.when(cnt == 1)
    def _():
        step(-1, 1, "plain", do_v=False, do_e=False)
        step(0, 0, 0, do_v=False)
        step(1, 1, None)
        step(2, 0, None, do_e=False)
        finish()

    @pl.when(cnt >= 2)
    def _():
        step(-1, 1, "plain", do_v=False, do_e=False)
        step(0, 0, "plain", do_v=False)

    unroll = DA_UNROLL

    def body(it, carry):
        for k in range(unroll):
            step(1 + unroll * it + k, (1 + k) & 1, "plain")
        return carry

    n_plain = jnp.maximum(cnt - 2, 0)
    lax.fori_loop(0, n_plain // unroll, body, 0)

    for rem in range(unroll):
        @pl.when((cnt >= 2) & (n_plain % unroll == rem))
        def _(rem=rem):
            for k in range(rem):
                step(cnt - 1 - rem + k, (1 + k) & 1, "plain")
            step(cnt - 1, (1 + rem) & 1, 0)
            step(cnt, rem & 1, None)
            step(cnt + 1, (1 + rem) & 1, None, do_e=False)
            finish()


def _diffattn(qk3, vt3, lq1, lk1, lq2, lk2, slopes, subln_w):
    b, s, _ = qk3.shape
    hh = DA_HEADS
    tq, tk = DA_TQ, DA_TK
    assert tq == tk and s % tq == 0
    nkt = s // tk
    c2 = lambda bi, h, i: (0, 0)
    return pl.pallas_call(
        _da_kernel,
        grid=(b, hh, s // tq),
        in_specs=[
            pl.BlockSpec((1, DA_HEAD_DIM), c2),
            pl.BlockSpec((1, DA_HEAD_DIM), c2),
            pl.BlockSpec((1, DA_HEAD_DIM), c2),
            pl.BlockSpec((1, DA_HEAD_DIM), c2),
            pl.BlockSpec((None, 1, LANES), lambda bi, h, i: (h, 0, 0)),
            pl.BlockSpec((None, tq, LANES), lambda bi, h, i: (bi, i, h)),
            pl.BlockSpec((None, s, LANES), lambda bi, h, i: (bi, 0, hh + h)),
            pl.BlockSpec((nkt, DA_V_AUG, tk), lambda bi, h, i: (bi, h, 0)),
            pl.BlockSpec((1, DA_V_DIM), c2),
        ],
        out_specs=pl.BlockSpec((None, tq, LANES), lambda bi, h, i: (bi, i, h)),
        out_shape=jax.ShapeDtypeStruct((b, s, DA_WIDTH), BF16),
        scratch_shapes=[
            pltpu.VMEM((nkt, tk, LANES), BF16),
            pltpu.VMEM((nkt, tk, LANES), BF16),
        ] + [pltpu.VMEM((tk, tq), F32)] * 4 + [pltpu.VMEM((tk, tq), BF16)] * 4 + [
            pltpu.VMEM((DA_V_AUG, tq), F32),
            pltpu.VMEM((DA_V_AUG, tq), F32),
            pltpu.VMEM((2, 8, tq), F32),
            pltpu.VMEM((8, LANES), F32),
        ],
        compiler_params=_cparams(("parallel", "parallel", "arbitrary")),
        name="diffattn",
    )(lq1, lk1, lq2, lk2, slopes, qk3, qk3, vt3, subln_w)


ROUTER_GROUP_LANE0 = N_EXPERTS
MIXER_ROW_PARTS = 2


def _mixer_kernel(x_ref, odn_ref, oda_ref, p_ref, eg_ref, eb_ref, wz_ref, bz_ref, wg_ref, bg_ref,
                  wdn_ref, wda_ref, wout_ref, g1_ref, b1_ref, wr_cat_ref, wr_hi_ref, br_ref,
                  wpg_ref, bpg_ref, wpp_ref, h1b_ref, base_ref, logits_ref):
    n_parts = MIXER_ROW_PARTS
    pm = x_ref.shape[0] // n_parts
    parts = [slice(r * pm, (r + 1) * pm) for r in range(n_parts)]
    st = [dict() for _ in parts]
    for s, rows in zip(st, parts):
        s["h"] = _layer_norm(x_ref[rows, :], eg_ref[...], eb_ref[...])
        s["hb"] = s["h"].astype(BF16)
    for s, rows in zip(st, parts):
        s["z"] = _dot(s["hb"], wz_ref[...]) + bz_ref[...]
        s["gl"] = _dot(s["hb"], wg_ref[...]) + bg_ref[...]
    for s, rows in zip(st, parts):
        s["y_dn"] = _dot((odn_ref[rows, :] * _silu(s["z"])).astype(BF16), wdn_ref[...])
        s["y_da"] = _dot(oda_ref[rows, :], wda_ref[...])
    for s, rows in zip(st, parts):
        gates = _sigmoid(s["gl"])
        mixed = gates[:, :D_MODEL] * s["y_dn"] + gates[:, D_MODEL:] * s["y_da"]
        s["mix"] = _dot(mixed.astype(BF16), wout_ref[...])
    for s, rows in zip(st, parts):
        s["h1"] = _layer_norm(ALPHA * s["h"] + s["mix"], g1_ref[...], b1_ref[...])
        s["h1b"] = s["h1"].astype(BF16)
        h1b_ref[rows, :] = s["h1b"]
    for s, rows in zip(st, parts):
        ple = (_sigmoid(_dot(s["h1b"], wpg_ref[...]) + bpg_ref[...])
               * _dot(p_ref[rows, :].astype(BF16), wpp_ref[...]))
        base_ref[rows, :] = ALPHA * s["h1"] + ple
        h1_lo = (s["h1"] - s["h1b"].astype(F32)).astype(BF16)
        both = _dot(s["h1b"], wr_cat_ref[...])
        s["logits"] = (both[:, :LANES] + both[:, LANES:] + _dot(h1_lo, wr_hi_ref[...]) + br_ref[...])
    for s, rows in zip(st, parts):
        logits_ref[rows, :] = s["logits"]


def _route(logits):
    tm = logits.shape[0]
    lane = lax.broadcasted_iota(jnp.int32, (tm, LANES), 1).astype(F32)
    big = float(LANES)
    is_g = (lane >= ROUTER_GROUP_LANE0) & (lane < ROUTER_GROUP_LANE0 + N_GROUPS)
    gl = jnp.where(is_g, logits, NEG_INF)
    gmax = jnp.max(gl, axis=-1, keepdims=True)
    gsum = jnp.sum(jnp.exp(gl - gmax), axis=-1, keepdims=True)
    g_val = 1.0 / gsum
    g_idx = jnp.min(jnp.where(gl == gmax, lane, big), axis=-1, keepdims=True) - ROUTER_GROUP_LANE0
    e_lo = g_idx * EXPERTS_PER_GROUP
    in_g = (lane >= e_lo) & (lane < e_lo + EXPERTS_PER_GROUP)
    el = jnp.where(in_g, logits, NEG_INF)
    emax = jnp.max(el, axis=-1, keepdims=True)
    ee = jnp.exp(el - emax)
    e_prob = ee / jnp.sum(ee, axis=-1, keepdims=True)
    p1 = jnp.max(e_prob, axis=-1, keepdims=True)
    i1 = jnp.min(jnp.where(in_g & (e_prob == p1), lane, big), axis=-1, keepdims=True)
    rest = jnp.where(in_g & (lane != i1), e_prob, -1.0)
    p2 = jnp.max(rest, axis=-1, keepdims=True)
    i2 = jnp.min(jnp.where(rest == p2, lane, big), axis=-1, keepdims=True)
    denom = p1 + p2
    return (jnp.where(lane == i1, p1 / denom * g_val, 0.0)
            + jnp.where(lane == i2, p2 / denom * g_val, 0.0))


def _mixer(x2, odn2, oda2, p2, ws, tm):
    n = x2.shape[0]
    row = lambda i: (i, 0)
    const = lambda i: (0, 0)

    def cspec(a):
        return pl.BlockSpec(a.shape, const, pipeline_mode=pl.Buffered(1))

    (eg, eb, wz, bz, wg, bg, wdn, wda, wout, g1, b1, wr_cat, wr_hi, br, wpg, bpg, wpp) = ws
    return pl.pallas_call(
        _mixer_kernel,
        grid=(n // tm,),
        in_specs=[
            pl.BlockSpec((tm, D_MODEL), row),
            pl.BlockSpec((tm, DN_WIDTH), row),
            pl.BlockSpec((tm, DA_WIDTH), row),
            pl.BlockSpec((tm, PLE_DIM), row),
        ] + [cspec(a) for a in ws],
        out_specs=[
            pl.BlockSpec((tm, D_MODEL), row),
            pl.BlockSpec((tm, D_MODEL), row),
            pl.BlockSpec((tm, LANES), row),
        ],
        out_shape=[
            jax.ShapeDtypeStruct((n, D_MODEL), BF16),
            jax.ShapeDtypeStruct((n, D_MODEL), F32),
            jax.ShapeDtypeStruct((n, LANES), F32),
        ],
        compiler_params=_cparams(("parallel",)),
        name="mixer",
    )(x2, odn2, oda2, p2, *ws)


def _moe_kernel(h_ref, base_ref, logits_ref, wg_ref, wu_ref, wd_ref, g2_ref, b2_ref, o_ref, acc_ref):
    g = pl.program_id(1)

    @pl.when(g == 0)
    def _():
        acc_ref[...] = base_ref[...]

    comb = _route(logits_ref[...])
    lane = lax.broadcasted_iota(jnp.int32, comb.shape, 1)
    h = h_ref[...]
    cols = []
    for e in range(EXPERTS_PER_GROUP):
        c = jnp.sum(jnp.where(lane == g * EXPERTS_PER_GROUP + e, comb, 0.0), axis=-1, keepdims=True)
        hid = _silu(_dot(h, wg_ref[e])) * _dot(h, wu_ref[e])
        cols.append((hid * c).astype(BF16))
    hid_all = jnp.concatenate(cols, axis=1)
    wd = wd_ref[...].reshape(EXPERTS_PER_GROUP * EXPERT_FF, D_MODEL)
    acc_ref[...] += _dot(hid_all, wd)

    @pl.when(g == N_GROUPS - 1)
    def _():
        o_ref[...] = _layer_norm(acc_ref[...], g2_ref[...], b2_ref[...])


def _moe(h1b, base, logits, wg, wu, wd, g2, b2, tm):
    n = h1b.shape[0]
    row = lambda i, g: (i, 0)
    return pl.pallas_call(
        _moe_kernel,
        grid=(n // tm, N_GROUPS),
        in_specs=[
            pl.BlockSpec((tm, D_MODEL), row),
            pl.BlockSpec((tm, D_MODEL), row),
            pl.BlockSpec((tm, LANES), row),
            pl.BlockSpec((EXPERTS_PER_GROUP, D_MODEL, EXPERT_FF), lambda i, g: (g, 0, 0)),
            pl.BlockSpec((EXPERTS_PER_GROUP, D_MODEL, EXPERT_FF), lambda i, g: (g, 0, 0)),
            pl.BlockSpec((EXPERTS_PER_GROUP, EXPERT_FF, D_MODEL), lambda i, g: (g, 0, 0)),
            pl.BlockSpec((1, D_MODEL), lambda i, g: (0, 0)),
            pl.BlockSpec((1, D_MODEL), lambda i, g: (0, 0)),
        ],
        out_specs=pl.BlockSpec((tm, D_MODEL), row),
        out_shape=jax.ShapeDtypeStruct((n, D_MODEL), F32),
        scratch_shapes=[pltpu.VMEM((tm, D_MODEL), F32)],
        compiler_params=_cparams(("parallel", "arbitrary")),
        name="moe",
    )(h1b, base, logits, wg, wu, wd, g2, b2)


def _row(v):
    return v.reshape(1, -1).astype(F32)


def _pad_lanes(v, offset=0):
    out = jnp.zeros((1, LANES), F32)
    return lax.dynamic_update_slice(out, v.reshape(1, -1).astype(F32), (0, offset))


def _tile(n, pref):
    t = min(pref, n)
    while n % t:
        t //= 2
    return t


def kernel(x, p, emb_ln_g, emb_ln_b, w_in, b_in, conv_w, dn_a_log, dn_dt_bias, dn_norm_w, w_dn_o,
           da_lq1, da_lk1, da_lq2, da_lk2, da_subln_w, w_da_o, w_out, ln1_g, ln1_b,
           w_router_group, b_router_group, w_router_expert, b_router_expert,
           w_exp_gate, w_exp_up, w_exp_down, w_ple_gate, b_ple_gate, w_ple_proj, ln2_g, ln2_b):
    B, S, _ = x.shape
    n = B * S
    assert S % CHUNK == 0
    li = 0
    offs = np.concatenate([[0], np.cumsum(IN_SIZES)]).astype(int)
    c_dn, c_z, c_b, c_a, c_q, c_k, c_v, c_g = [slice(int(offs[j]), int(offs[j + 1])) for j in range(8)]
    w_i, b_i = w_in[li], b_in[li]

    small_w = jnp.zeros((D_MODEL, LANES), F32)
    small_w = small_w.at[:, 0:DN_HEADS].set(w_i[:, c_b]).at[:, DN_HEADS:2 * DN_HEADS].set(w_i[:, c_a])
    small_b = jnp.zeros((LANES,), F32)
    small_b = small_b.at[0:DN_HEADS].set(b_i[c_b]).at[DN_HEADS:2 * DN_HEADS].set(b_i[c_a])
    w_a = jnp.concatenate([w_i[:, c_dn], small_w, w_i[:, c_q], w_i[:, c_k], w_i[:, c_v]], axis=1).astype(BF16)
    b_a = jnp.concatenate([b_i[c_dn], small_b, b_i[c_q], b_i[c_k], b_i[c_v]]).reshape(1, -1).astype(F32)
    alog = _pad_lanes(dn_a_log[li], DN_HEADS)
    dtb = _pad_lanes(dn_dt_bias[li], DN_HEADS)

    x2 = x.reshape(n, D_MODEL)
    tm_a = _tile(n, 512)
    dn_raw, ba, bat, qk, vt = _in_proj(x2, _row(emb_ln_g), _row(emb_ln_b), w_a, b_a, alog, dtb,
                                  conv_w[li].astype(F32), tm_a, S)

    tb = _tile(S, 1024)
    o_dn = _deltanet(dn_raw.reshape(B, S, DN_COLS), ba.reshape(B, S, LANES), bat, _row(dn_norm_w[li]), tb)

    slopes = jnp.asarray(2.0 ** (-8.0 * np.arange(1, DA_HEADS + 1) / DA_HEADS), dtype=F32)
    slopes = jnp.broadcast_to(slopes[:, None, None], (DA_HEADS, 1, LANES))
    o_da = _diffattn(qk.reshape(B, S, 2 * DA_QK_WIDTH), vt, _row(da_lq1[li]), _row(da_lk1[li]),
                      _row(da_lq2[li]), _row(da_lk2[li]), slopes, _row(da_subln_w[li]))

    wr = jnp.zeros((D_MODEL, LANES), F32)
    wr = wr.at[:, 0:N_EXPERTS].set(w_router_expert[li])
    wr = wr.at[:, ROUTER_GROUP_LANE0:ROUTER_GROUP_LANE0 + N_GROUPS].set(w_router_group[li])
    wr_hi = wr.astype(BF16)
    wr_lo = (wr - wr_hi.astype(F32)).astype(BF16)
    br = jnp.zeros((LANES,), F32).at[0:N_EXPERTS].set(b_router_expert[li])
    br = br.at[ROUTER_GROUP_LANE0:ROUTER_GROUP_LANE0 + N_GROUPS].set(b_router_group[li]).reshape(1, LANES)
    ws = (_row(emb_ln_g), _row(emb_ln_b),
          w_i[:, c_z].astype(BF16), _row(b_i[c_z]),
          w_i[:, c_g].astype(BF16), _row(b_i[c_g]),
          w_dn_o[li].astype(BF16), w_da_o[li].astype(BF16), w_out[li].astype(BF16),
          _row(ln1_g[li]), _row(ln1_b[li]), jnp.concatenate([wr_hi, wr_lo], axis=1), wr_hi, br,
          w_ple_gate[li].astype(BF16), _row(b_ple_gate[li]), w_ple_proj[li].astype(BF16))
    tm_c = _tile(n, 512)
    h1b, base, logits = _mixer(x2, o_dn.reshape(n, DN_WIDTH), o_da.reshape(n, DA_WIDTH),
                             p[li].reshape(n, PLE_DIM), ws, tm_c)

    wg = w_exp_gate[li].astype(BF16)
    wu = w_exp_up[li].astype(BF16)
    wd = w_exp_down[li].astype(BF16)
    tm_d = _tile(n, 1024)
    out = _moe(h1b, base, logits, wg, wu, wd, _row(ln2_g[li]), _row(ln2_b[li]), tm_d)
    return out.reshape(B, S, D_MODEL)
```

```python
import functools
import math

import jax
import jax.numpy as jnp
import numpy as np
from jax import lax
from jax.experimental import pallas as pl
from jax.experimental.pallas import tpu as pltpu

F32 = jnp.float32
BF16 = jnp.bfloat16

D_MODEL = 1024
PLE_DIM = 256
DN_HEADS = 4
DN_HEAD_DIM = 128
DN_WIDTH = DN_HEADS * DN_HEAD_DIM
CONV_WIDTH = 4
CHUNK = 64
DA_HEADS = 4
DA_HEAD_DIM = 64
DA_V_DIM = 2 * DA_HEAD_DIM
DA_QK_WIDTH = DA_HEADS * 2 * DA_HEAD_DIM
DA_WIDTH = DA_HEADS * DA_V_DIM
N_GROUPS = 4
EXPERTS_PER_GROUP = 4
N_EXPERTS = N_GROUPS * EXPERTS_PER_GROUP
EXPERT_FF = 256
IN_SIZES = (3 * DN_WIDTH, DN_WIDTH, DN_HEADS, DN_HEADS, DA_QK_WIDTH, DA_QK_WIDTH, DA_WIDTH, 2 * D_MODEL)
LN_EPS = 1e-5
RMS_EPS = 1e-6
DEPTH = 1
ALPHA = (2.0 * DEPTH) ** 0.25
LAM_INIT = 0.8 - 0.6 * math.exp(-0.3 * 0)

LANES = 128
VMEM_LIMIT = 56 * 1024 * 1024

NEG_INF = float("-inf")


def _cparams(sem, flags=None):
    return pltpu.CompilerParams(dimension_semantics=sem, vmem_limit_bytes=VMEM_LIMIT, flags=flags)


def _layer_norm(x, g, b):
    mu = jnp.mean(x, axis=-1, keepdims=True)
    xc = x - mu
    var = jnp.mean(xc * xc, axis=-1, keepdims=True)
    return xc * lax.rsqrt(var + LN_EPS) * g + b


def _sigmoid(x):
    return 1.0 / (1.0 + jnp.exp2(x * (-LOG2E)))


def _silu(x):
    return x * _sigmoid(x)


def _dot(a, b):
    return jnp.dot(a, b, preferred_element_type=F32)


def _dot_nt(a, b):
    return lax.dot_general(a, b, (((1,), (1,)), ((), ())), preferred_element_type=F32)


DN_COLS = 3 * DN_WIDTH
DA_V_PAD = 16
DA_V_AUG = DA_V_DIM + DA_V_PAD
DA_TK = 512
DA_TQ = DA_TK
LOG2E = 1.4426950408889634


IN_PROJ_ROW_PARTS = 2
TAIL = 8
BAT_ROWS = 8


def _in_proj_kernel(tiles_per_seq, x_ref, g_ref, b_ref, w_ref, bias_ref, alog_ref, dtb_ref, cw_ref,
                    dn_ref, ba_ref, bat_ref, qk_ref, vt_ref, xe_ref):
    tm = x_ref.shape[0]
    assert vt_ref.shape[0] == 1 and tm == DA_TK
    n_parts = IN_PROJ_ROW_PARTS
    pm = tm // n_parts
    parts = [slice(r * pm, (r + 1) * pm) for r in range(n_parts)]
    c0 = DN_COLS + LANES
    qw = DA_QK_WIDTH

    def proj(hb, lo, hi):
        return _dot(hb, w_ref[:, lo:hi]) + bias_ref[:, lo:hi]

    @pl.when(pl.program_id(0) % tiles_per_seq == 0)
    def _():
        xe_ref[tm:tm + TAIL, :] = jnp.zeros((TAIL, DN_COLS), F32)

    xe_ref[0:TAIL, :] = xe_ref[tm:tm + TAIL, :]
    hbs = [_layer_norm(x_ref[rows, :], g_ref[...], b_ref[...]).astype(BF16) for rows in parts]
    for hb, rows in zip(hbs, parts):
        xe_ref[TAIL + rows.start:TAIL + rows.stop, :] = proj(hb, 0, DN_COLS)
    for hb, rows in zip(hbs, parts):
        qk_ref[rows, :qw] = (proj(hb, c0, c0 + qw) * (DA_HEAD_DIM ** -0.5 * LOG2E)).astype(BF16)
        qk_ref[rows, qw:] = proj(hb, c0 + qw, c0 + 2 * qw).astype(BF16)
    for hb, rows in zip(hbs, parts):
        v_t = proj(hb, c0 + 2 * qw, c0 + 3 * qw).T.astype(BF16)
        extra = (lax.broadcasted_iota(jnp.int32, (DA_V_PAD, pm), 0) == 0).astype(F32).astype(BF16)
        vt_ref[0, :, rows] = jnp.concatenate(
            [blk for hd in range(DA_HEADS) for blk in (v_t[hd * DA_V_DIM:(hd + 1) * DA_V_DIM], extra)], axis=0)
    for hb, rows in zip(hbs, parts):
        ba = proj(hb, DN_COLS, DN_COLS + LANES)
        lane = lax.broadcasted_iota(jnp.int32, (pm, LANES), 1)
        row = lax.broadcasted_iota(jnp.int32, (pm, LANES), 0)
        beta = _sigmoid(ba)
        xs = ba + dtb_ref[...]
        softplus = jnp.maximum(xs, 0.0) + jnp.log(1.0 + jnp.exp(-jnp.abs(xs)))
        g = -jnp.exp(alog_ref[...]) * softplus
        g = jnp.where((lane >= DN_HEADS) & (lane < 2 * DN_HEADS), g, 0.0)
        rin = row & (CHUNK - 1)
        k = 1
        while k < CHUNK:
            g = g + jnp.where(rin >= k, pltpu.roll(g, k, 0), 0.0)
            k *= 2
        ba_out = jnp.where(lane < DN_HEADS, beta, g)
        ba_ref[rows, :] = ba_out
        bat_ref[:, rows] = ba_out.T[0:BAT_ROWS, :]

    cw = cw_ref[...]
    hd = DN_HEAD_DIM
    for rows in parts:
        for hh in range(3 * DN_HEADS):
            cols = slice(hh * hd, (hh + 1) * hd)
            r0 = TAIL + rows.start
            xh = xe_ref[r0:r0 + pm, cols] * cw[CONV_WIDTH - 1:CONV_WIDTH, cols]
            for j in range(CONV_WIDTH - 1):
                off = r0 - (CONV_WIDTH - 1) + j
                xh = xh + xe_ref[off:off + pm, cols] * cw[j:j + 1, cols]
            xh = _silu(xh)
            if hh < 2 * DN_HEADS:
                scale = (hd ** -0.5) if hh < DN_HEADS else 1.0
                xh = xh * lax.rsqrt(jnp.sum(xh * xh, axis=-1, keepdims=True) + RMS_EPS) * scale
            dn_ref[rows, cols] = xh


def _in_proj(x2, emb_g, emb_b, w_a, b_a, alog, dtb, conv_w, tm, seq):
    n = x2.shape[0]
    wcols = w_a.shape[1]
    assert seq % tm == 0
    const = lambda i: (0, 0)
    return pl.pallas_call(
        functools.partial(_in_proj_kernel, seq // tm),
        grid=(n // tm,),
        in_specs=[
            pl.BlockSpec((tm, D_MODEL), lambda i: (i, 0)),
            pl.BlockSpec((1, D_MODEL), const),
            pl.BlockSpec((1, D_MODEL), const),
            pl.BlockSpec((D_MODEL, wcols), const),
            pl.BlockSpec((1, wcols), const),
            pl.BlockSpec((1, LANES), const),
            pl.BlockSpec((1, LANES), const),
            pl.BlockSpec((CONV_WIDTH, DN_COLS), const),
        ],
        out_specs=[
            pl.BlockSpec((tm, DN_COLS), lambda i: (i, 0)),
            pl.BlockSpec((tm, LANES), lambda i: (i, 0)),
            pl.BlockSpec((BAT_ROWS, tm), lambda i: (0, i)),
            pl.BlockSpec((tm, 2 * DA_QK_WIDTH), lambda i: (i, 0)),
            pl.BlockSpec((tm // DA_TK, DA_HEADS * DA_V_AUG, DA_TK), lambda i: (i, 0, 0)),
        ],
        out_shape=[
            jax.ShapeDtypeStruct((n, DN_COLS), F32),
            jax.ShapeDtypeStruct((n, LANES), F32),
            jax.ShapeDtypeStruct((BAT_ROWS, n), F32),
            jax.ShapeDtypeStruct((n, 2 * DA_QK_WIDTH), BF16),
            jax.ShapeDtypeStruct((n // DA_TK, DA_HEADS * DA_V_AUG, DA_TK), BF16),
        ],
        scratch_shapes=[pltpu.VMEM((tm + 2 * TAIL, DN_COLS), F32)],
        compiler_params=_cparams(("arbitrary",)),
        name="in_proj",
    )(x2, emb_g, emb_b, w_a, b_a, alog, dtb, conv_w)


DN_GROUP_CHUNKS = 4


def _dn_kernel(x_ref, ba_ref, gt_ref, nw_ref, o_ref, state_ref):
    sblk = pl.program_id(1)
    tb = x_ref.shape[0]
    nck = tb // CHUNK
    hh = DN_HEADS
    hd = DN_HEAD_DIM

    @pl.when(sblk == 0)
    def _():
        state_ref[...] = jnp.zeros_like(state_ref)

    ba = ba_ref[...]
    ri = lax.broadcasted_iota(jnp.int32, (CHUNK, CHUNK), 0)
    ci = lax.broadcasted_iota(jnp.int32, (CHUNK, CHUNK), 1)
    causal = ri >= ci
    strict = ri > ci
    eye = jnp.where(ri == ci, 1.0, 0.0).astype(F32)
    nw = nw_ref[...]

    q_b, k_b, kb_b, rhs_uw, qg_b, kgt_b, decay, eg_last = {}, {}, {}, {}, {}, {}, {}, {}
    for h in range(hh):
        q = x_ref[:, h * hd:(h + 1) * hd]
        k = x_ref[:, (hh + h) * hd:(hh + h + 1) * hd]
        v = x_ref[:, (2 * hh + h) * hd:(2 * hh + h + 1) * hd]
        bcol = ba[:, h:h + 1]
        gcol = ba[:, hh + h:hh + h + 1]
        grow = gt_ref[hh + h:hh + h + 1, :]
        egcol = jnp.exp(gcol)
        kb = k * bcol
        uw = jnp.concatenate([v * bcol, kb * egcol], axis=1).astype(BF16)
        qg = (q * egcol).astype(BF16)
        qb = q.astype(BF16)
        kbb = kb.astype(BF16)
        kk = k.astype(BF16)
        for c in range(nck):
            sl = slice(c * CHUNK, (c + 1) * CHUNK)
            gc = gcol[sl]
            g_last = gc[CHUNK - 1:CHUNK, :]
            q_b[h, c], k_b[h, c], kb_b[h, c] = qb[sl], kk[sl], kbb[sl]
            rhs_uw[h, c], qg_b[h, c] = uw[sl], qg[sl]
            kgt_b[h, c] = (k[sl] * jnp.exp(g_last - gc)).T.astype(BF16)
            decay[h, c] = jnp.exp(jnp.where(causal, gc - grow[:, sl], NEG_INF))
            eg_last[h, c] = jnp.exp(g_last)

    power, t, a_intra, u, w_b = {}, {}, {}, {}, {}
    n_rounds = int(math.log2(CHUNK)) - 1

    def local_stages(group):
        def s_nmat():
            for p_ in group:
                nmat = jnp.where(strict, -_dot_nt(kb_b[p_], k_b[p_]) * decay[p_], 0.0)
                power[p_] = nmat
                t[p_] = eye + nmat

        def s_intra():
            for p_ in group:
                a_intra[p_] = (_dot_nt(q_b[p_], k_b[p_]) * decay[p_]).astype(BF16)

        def s_square():
            for p_ in group:
                pb = power[p_].astype(BF16)
                power[p_] = _dot(pb, pb)

        def s_round(last):
            for p_ in group:
                pb = power[p_].astype(BF16)
                if last:
                    t[p_] = t[p_] + _dot(t[p_].astype(BF16), pb)
                else:
                    both = _dot(jnp.concatenate([power[p_], t[p_]], axis=0).astype(BF16), pb)
                    power[p_] = both[:CHUNK]
                    t[p_] = t[p_] + both[CHUNK:]

        def s_uw():
            for p_ in group:
                uw = _dot(t[p_].astype(BF16), rhs_uw[p_])
                u[p_] = uw[:, :hd]
                w_b[p_] = uw[:, hd:].astype(BF16)

        rounds = [functools.partial(s_round, r == n_rounds - 1) for r in range(n_rounds)]
        return [s_nmat, s_intra, s_square] + rounds + [s_uw]

    states = [state_ref[h] for h in range(hh)]
    ws = {}

    def recur_stages(chunks):
        def r_first(c):
            for h in range(hh):
                lhs = jnp.concatenate([w_b[h, c], qg_b[h, c]], axis=0)
                ws[h] = _dot(lhs, states[h].astype(BF16))

        def r_second(c):
            for h in range(hh):
                v_new = (u[h, c] - ws[h][:CHUNK]).astype(BF16)
                lhs = jnp.concatenate([a_intra[h, c], kgt_b[h, c]], axis=0)
                both = _dot(lhs, v_new)
                o_c = ws[h][CHUNK:] + both[:CHUNK]
                states[h] = states[h] * eg_last[h, c] + both[CHUNK:]
                o_n = o_c * lax.rsqrt(jnp.mean(o_c * o_c, axis=-1, keepdims=True) + RMS_EPS) * nw
                o_ref[c * CHUNK:(c + 1) * CHUNK, h * hd:(h + 1) * hd] = o_n

        return [functools.partial(f, c) for c in chunks for f in (r_first, r_second)]

    gsz = min(DN_GROUP_CHUNKS, nck)
    groups = [list(range(g0, g0 + gsz)) for g0 in range(0, nck, gsz)]
    pending = []
    for chunks in groups:
        local = local_stages([(h, c) for c in chunks for h in range(hh)])
        for k, stage in enumerate(local):
            stage()
            if pending:
                pending.pop(0)()
        while pending:
            pending.pop(0)()
        pending = recur_stages(chunks)
    while pending:
        pending.pop(0)()
    for h in range(hh):
        state_ref[h] = states[h]


def _deltanet(dn3, ba3, bat, norm_w, tb):
    b, s, _ = dn3.shape
    hh = DN_HEADS
    nblk = s // tb
    return pl.pallas_call(
        _dn_kernel,
        grid=(b, s // tb),
        in_specs=[
            pl.BlockSpec((None, tb, DN_COLS), lambda bi, i: (bi, i, 0)),
            pl.BlockSpec((None, tb, LANES), lambda bi, i: (bi, i, 0)),
            pl.BlockSpec((BAT_ROWS, tb), lambda bi, i: (0, bi * nblk + i)),
            pl.BlockSpec((1, LANES), lambda bi, i: (0, 0)),
        ],
        out_specs=pl.BlockSpec((None, tb, DN_WIDTH), lambda bi, i: (bi, i, 0)),
        out_shape=jax.ShapeDtypeStruct((b, s, DN_WIDTH), F32),
        scratch_shapes=[pltpu.VMEM((hh, DN_HEAD_DIM, DN_HEAD_DIM), F32)],
        compiler_params=_cparams(("parallel", "arbitrary")),
        name="deltanet",
    )(dn3, ba3, bat, norm_w)


N_BIAS_LANES = 3
DA_UNROLL = 2


def _da_kernel(lq1_ref, lk1_ref, lq2_ref, lk2_ref, slope_ref, q_ref, k_ref, vt_ref, w_ref, o_ref,
                ka1_ref, ka2_ref, *scr):
    i = pl.program_id(2)
    tq = q_ref.shape[0]
    s_refs = ((scr[0], scr[1]), (scr[2], scr[3]))
    p_refs = ((scr[4], scr[5]), (scr[6], scr[7]))
    acc_refs = (scr[8], scr[9])
    st_ref = scr[10]
    kn_ref = scr[11]
    feat_ref = scr[12]
    nkt, tk, _ = ka1_ref.shape
    half = DA_HEAD_DIM

    @pl.when((pl.program_id(1) == 0) & (i == 0))
    def _():
        c = slope_ref[...] * LOG2E
        lane = lax.broadcasted_iota(jnp.int32, (tk, LANES), 1)
        row = lax.broadcasted_iota(jnp.int32, (tk, LANES), 0)

        def build_feat(t, carry):
            bias = c * (t * tk + row).astype(F32)
            hi = bias.astype(BF16).astype(F32)
            rem = bias - hi
            mid = rem.astype(BF16).astype(F32)
            lo = rem - mid
            piece = lane & (half - 1)
            feat = jnp.where(piece == 0, hi, jnp.where(piece == 1, mid, jnp.where(piece == 2, lo, 0.0)))
            feat_ref[t] = feat.astype(BF16)
            return carry

        lax.fori_loop(0, nkt, build_feat, 0)

    @pl.when(i == 0)
    def _():
        lane = lax.broadcasted_iota(jnp.int32, (tk, LANES), 1)

        def build(t, carry):
            kb = k_ref[pl.ds(pl.multiple_of(t * tk, tk), tk), :].astype(F32)
            feat = feat_ref[t].astype(F32)
            ka1_ref[t] = jnp.where(lane < half, kb, feat).astype(BF16)
            ka2_ref[t] = jnp.where(lane >= half, kb, feat).astype(BF16)
            sq = kb * kb
            n1 = jnp.max(jnp.sum(jnp.where(lane < half, sq, 0.0), axis=-1, keepdims=True), axis=0, keepdims=True)
            n2 = jnp.max(jnp.sum(jnp.where(lane >= half, sq, 0.0), axis=-1, keepdims=True), axis=0, keepdims=True)
            kn1, kn2 = carry
            return jnp.where(tile_lane == t, n1, kn1), jnp.where(tile_lane == t, n2, kn2)

        tile_lane = lax.broadcasted_iota(jnp.int32, (1, LANES), 1)
        zeros = jnp.zeros((1, LANES), F32)
        kn1, kn2 = lax.fori_loop(0, nkt, build, (zeros, zeros))
        kn_ref[0:1, :] = kn1
        kn_ref[1:2, :] = kn2

    qv = q_ref[...].astype(F32)
    lane_q = lax.broadcasted_iota(jnp.int32, qv.shape, 1)
    q1 = jnp.where(lane_q < half, qv, jnp.where(lane_q < half + N_BIAS_LANES, 1.0, 0.0))
    q2 = jnp.where(lane_q >= half, qv, jnp.where(lane_q < N_BIAS_LANES, 1.0, 0.0))
    qts = (q1.T.astype(BF16), q2.T.astype(BF16))
    kas = (ka1_ref, ka2_ref)

    SKIP_BELOW = -170.0
    NORM_SLACK = 1.01
    qsq = qv * qv
    qn = [jnp.sqrt(jnp.max(jnp.sum(jnp.where(msk, qsq, 0.0), axis=-1, keepdims=True), axis=0, keepdims=True))
          for msk in (lane_q < half, lane_q >= half)]
    tile_lane = lax.broadcasted_iota(jnp.int32, (1, LANES), 1)
    gap = ((i - tile_lane - 1) * tk + 1).astype(F32)
    c_row = slope_ref[...] * LOG2E
    bound = None
    for mp in range(2):
        kn = jnp.sqrt(kn_ref[mp:mp + 1, :])
        kd = jnp.max(jnp.where(tile_lane == i, kn, 0.0), axis=-1, keepdims=True)
        b_mp = qn[mp] * (kn + kd) * NORM_SLACK - c_row * gap
        bound = b_mp if bound is None else jnp.maximum(bound, b_mp)
    skippable = (bound < SKIP_BELOW) & (tile_lane < i)
    first_keep = jnp.min(jnp.where(skippable, float(LANES), tile_lane.astype(F32)), axis=-1, keepdims=True)
    base = first_keep.astype(jnp.int32)[0, 0]
    cnt = i - base

    M_ROW, A_ROW = 0, 2
    CH = 128
    SUB = 8
    V_AFTER_CHUNK = DA_TK // CH - 1

    def row(mp, r):
        return st_ref[mp, r:r + 1, :]

    def fold(x, op):
        return op(x.reshape(CH // SUB, SUB, tq), axis=0)

    def step(t, slot, s_mode, do_v=True, do_e=True):
        o = 1 - slot
        a_prevs = [row(mp, A_ROW + o) for mp in range(2)] if do_v else None
        for mp in range(2):
            m_t = row(mp, M_ROW + slot)
            tmax = jnp.full((SUB, tq), NEG_INF, F32)
            for c in range(tk // CH):
                rows = slice(c * CH, (c + 1) * CH)
                if do_e:
                    pc = jnp.exp2(s_refs[slot][mp][rows, :] - m_t)
                    p_refs[slot][mp][rows, :] = pc.astype(BF16)
                if s_mode is not None:
                    sc = _dot(kas[mp][base + t + 1, rows, :], qts[mp])
                    if s_mode != "plain":
                        krow = lax.broadcasted_iota(jnp.int32, (CH, tq), 0) + (c * CH + s_mode)
                        qcol = lax.broadcasted_iota(jnp.int32, (CH, tq), 1)
                        sc = jnp.where(krow <= qcol, sc, NEG_INF)
                    s_refs[o][mp][rows, :] = sc
                    tmax = jnp.maximum(tmax, fold(sc, jnp.max))
                if do_v and c == V_AFTER_CHUNK:
                    vt = vt_ref[base + t - 1]
                    acc_refs[mp][...] = a_prevs[mp] * acc_refs[mp][...] + _dot(vt, p_refs[o][mp][...])
            if s_mode is not None:
                m_new = jnp.maximum(m_t, jnp.max(tmax, axis=0, keepdims=True))
                st_ref[mp, M_ROW + o:M_ROW + o + 1, :] = m_new
                st_ref[mp, A_ROW + o:A_ROW + o + 1, :] = jnp.exp2(m_t - m_new)

    for mp in range(2):
        acc_refs[mp][...] = jnp.zeros((DA_V_AUG, tq), F32)
        st_ref[mp, M_ROW + 1:M_ROW + 2, :] = jnp.full((1, tq), -1e30, F32)

    def finish():
        l1 = acc_refs[0][DA_V_DIM:DA_V_DIM + 1, :]
        l2 = acc_refs[1][DA_V_DIM:DA_V_DIM + 1, :]
        a1 = acc_refs[0][0:DA_V_DIM, :]
        a2 = acc_refs[1][0:DA_V_DIM, :]
        lam = (jnp.exp(jnp.sum(lq1_ref[...] * lk1_ref[...], axis=-1, keepdims=True))
               - jnp.exp(jnp.sum(lq2_ref[...] * lk2_ref[...], axis=-1, keepdims=True)) + LAM_INIT)
        o_t = a1 / l1 - lam * (a2 / l2)
        o_t = o_t * lax.rsqrt(jnp.mean(o_t * o_t, axis=0, keepdims=True) + RMS_EPS) * (1.0 - LAM_INIT)
        o_ref[...] = (o_t.T * w_ref[...]).astype(BF16)

    @pl.when(cnt == 0)
    def _():
        step(-1, 1, 0, do_v=False, do_e=False)
        step(0, 0, None, do_v=False)
        step(1, 1, None, do_e=False)
        finish()

    @pl.when(cnt == 1)
    def _():
        step(-1, 1, "plain", do_v=False, do_e=False)
        step(0, 0, 0, do_v=False)
        step(1, 1, None)
        step(2, 0, None, do_e=False)
        finish()

    @pl.when(cnt >= 2)
    def _():
        step(-1, 1, "plain", do_v=False, do_e=False)
        step(0, 0, "plain", do_v=False)

    unroll = DA_UNROLL

    def body(it, carry):
        for k in range(unroll):
            step(1 + unroll * it + k, (1 + k) & 1, "plain")
        return carry

    n_plain = jnp.maximum(cnt - 2, 0)
    lax.fori_loop(0, n_plain // unroll, body, 0)

    for rem in range(unroll):
        @pl.when((cnt >= 2) & (n_plain % unroll == rem))
        def _(rem=rem):
            for k in range(rem):
                step(cnt - 1 - rem + k, (1 + k) & 1, "plain")
            step(cnt - 1, (1 + rem) & 1, 0)
            step(cnt, rem & 1, None)
            step(cnt + 1, (1 + rem) & 1, None, do_e=False)
            finish()


def _diffattn(qk3, vt3, lq1, lk1, lq2, lk2, slopes, subln_w):
    b, s, _ = qk3.shape
    hh = DA_HEADS
    tq, tk = DA_TQ, DA_TK
    assert tq == tk and s % tq == 0
    nkt = s // tk
    c2 = lambda h, bi, i: (0, 0)
    return pl.pallas_call(
        _da_kernel,
        grid=(hh, b, s // tq),
        in_specs=[
            pl.BlockSpec((1, DA_HEAD_DIM), c2),
            pl.BlockSpec((1, DA_HEAD_DIM), c2),
            pl.BlockSpec((1, DA_HEAD_DIM), c2),
            pl.BlockSpec((1, DA_HEAD_DIM), c2),
            pl.BlockSpec((None, 1, LANES), lambda h, bi, i: (h, 0, 0)),
            pl.BlockSpec((None, tq, LANES), lambda h, bi, i: (bi, i, h)),
            pl.BlockSpec((None, s, LANES), lambda h, bi, i: (bi, 0, hh + h)),
            pl.BlockSpec((nkt, DA_V_AUG, tk), lambda h, bi, i: (bi, h, 0)),
            pl.BlockSpec((1, DA_V_DIM), c2),
        ],
        out_specs=pl.BlockSpec((None, tq, LANES), lambda h, bi, i: (bi, i, h)),
        out_shape=jax.ShapeDtypeStruct((b, s, DA_WIDTH), BF16),
        scratch_shapes=[
            pltpu.VMEM((nkt, tk, LANES), BF16),
            pltpu.VMEM((nkt, tk, LANES), BF16),
        ] + [pltpu.VMEM((tk, tq), F32)] * 4 + [pltpu.VMEM((tk, tq), BF16)] * 4 + [
            pltpu.VMEM((DA_V_AUG, tq), F32),
            pltpu.VMEM((DA_V_AUG, tq), F32),
            pltpu.VMEM((2, 8, tq), F32),
            pltpu.VMEM((8, LANES), F32),
            pltpu.VMEM((nkt, tk, LANES), BF16),
        ],
        compiler_params=_cparams(("arbitrary", "arbitrary", "arbitrary")),
        name="diffattn",
    )(lq1, lk1, lq2, lk2, slopes, qk3, qk3, vt3, subln_w)


ROUTER_GROUP_LANE0 = N_EXPERTS
MIXER_ROW_PARTS = 2


def _mixer_kernel(x_ref, odn_ref, oda_ref, p_ref, eg_ref, eb_ref, wz_ref, bz_ref, wg_ref, bg_ref,
                  wdn_ref, wda_ref, wout_ref, g1_ref, b1_ref, wr_cat_ref, wr_hi_ref, br_ref,
                  wpg_ref, bpg_ref, wpp_ref, h1b_ref, base_ref, logits_ref):
    n_parts = MIXER_ROW_PARTS
    pm = x_ref.shape[0] // n_parts
    parts = [slice(r * pm, (r + 1) * pm) for r in range(n_parts)]
    st = [dict() for _ in parts]
    for s, rows in zip(st, parts):
        s["h"] = _layer_norm(x_ref[rows, :], eg_ref[...], eb_ref[...])
        s["hb"] = s["h"].astype(BF16)
    for s, rows in zip(st, parts):
        s["z"] = _dot(s["hb"], wz_ref[...]) + bz_ref[...]
        s["gl"] = _dot(s["hb"], wg_ref[...]) + bg_ref[...]
    for s, rows in zip(st, parts):
        s["y_dn"] = _dot((odn_ref[rows, :] * _silu(s["z"])).astype(BF16), wdn_ref[...])
        s["y_da"] = _dot(oda_ref[rows, :], wda_ref[...])
    for s, rows in zip(st, parts):
        gates = _sigmoid(s["gl"])
        mixed = gates[:, :D_MODEL] * s["y_dn"] + gates[:, D_MODEL:] * s["y_da"]
        s["mix"] = _dot(mixed.astype(BF16), wout_ref[...])
    for s, rows in zip(st, parts):
        s["h1"] = _layer_norm(ALPHA * s["h"] + s["mix"], g1_ref[...], b1_ref[...])
        s["h1b"] = s["h1"].astype(BF16)
        h1b_ref[rows, :] = s["h1b"]
    for s, rows in zip(st, parts):
        ple = (_sigmoid(_dot(s["h1b"], wpg_ref[...]) + bpg_ref[...])
               * _dot(p_ref[rows, :].astype(BF16), wpp_ref[...]))
        base_ref[rows, :] = ALPHA * s["h1"] + ple
        h1_lo = (s["h1"] - s["h1b"].astype(F32)).astype(BF16)
        both = _dot(s["h1b"], wr_cat_ref[...])
        s["logits"] = (both[:, :LANES] + both[:, LANES:] + _dot(h1_lo, wr_hi_ref[...]) + br_ref[...])
    for s, rows in zip(st, parts):
        logits_ref[rows, :] = s["logits"]


def _route(logits):
    tm = logits.shape[0]
    lane = lax.broadcasted_iota(jnp.int32, (tm, LANES), 1).astype(F32)
    big = float(LANES)
    is_g = (lane >= ROUTER_GROUP_LANE0) & (lane < ROUTER_GROUP_LANE0 + N_GROUPS)
    gl = jnp.where(is_g, logits, NEG_INF)
    gmax = jnp.max(gl, axis=-1, keepdims=True)
    gsum = jnp.sum(jnp.exp(gl - gmax), axis=-1, keepdims=True)
    g_val = 1.0 / gsum
    g_idx = jnp.min(jnp.where(gl == gmax, lane, big), axis=-1, keepdims=True) - ROUTER_GROUP_LANE0
    e_lo = g_idx * EXPERTS_PER_GROUP
    in_g = (lane >= e_lo) & (lane < e_lo + EXPERTS_PER_GROUP)
    el = jnp.where(in_g, logits, NEG_INF)
    emax = jnp.max(el, axis=-1, keepdims=True)
    ee = jnp.exp(el - emax)
    e_prob = ee / jnp.sum(ee, axis=-1, keepdims=True)
    p1 = jnp.max(e_prob, axis=-1, keepdims=True)
    i1 = jnp.min(jnp.where(in_g & (e_prob == p1), lane, big), axis=-1, keepdims=True)
    rest = jnp.where(in_g & (lane != i1), e_prob, -1.0)
    p2 = jnp.max(rest, axis=-1, keepdims=True)
    i2 = jnp.min(jnp.where(rest == p2, lane, big), axis=-1, keepdims=True)
    denom = p1 + p2
    return (jnp.where(lane == i1, p1 / denom * g_val, 0.0)
            + jnp.where(lane == i2, p2 / denom * g_val, 0.0))


def _mixer(x2, odn2, oda2, p2, ws, tm):
    n = x2.shape[0]
    row = lambda i: (i, 0)
    const = lambda i: (0, 0)

    def cspec(a):
        return pl.BlockSpec(a.shape, const, pipeline_mode=pl.Buffered(1))

    (eg, eb, wz, bz, wg, bg, wdn, wda, wout, g1, b1, wr_cat, wr_hi, br, wpg, bpg, wpp) = ws
    return pl.pallas_call(
        _mixer_kernel,
        grid=(n // tm,),
        in_specs=[
            pl.BlockSpec((tm, D_MODEL), row),
            pl.BlockSpec((tm, DN_WIDTH), row),
            pl.BlockSpec((tm, DA_WIDTH), row),
            pl.BlockSpec((tm, PLE_DIM), row),
        ] + [cspec(a) for a in ws],
        out_specs=[
            pl.BlockSpec((tm, D_MODEL), row),
            pl.BlockSpec((tm, D_MODEL), row),
            pl.BlockSpec((tm, LANES), row),
        ],
        out_shape=[
            jax.ShapeDtypeStruct((n, D_MODEL), BF16),
            jax.ShapeDtypeStruct((n, D_MODEL), F32),
            jax.ShapeDtypeStruct((n, LANES), F32),
        ],
        compiler_params=_cparams(("parallel",)),
        name="mixer",
    )(x2, odn2, oda2, p2, *ws)


def _moe_kernel(h_ref, base_ref, logits_ref, wg_ref, wu_ref, wd_ref, g2_ref, b2_ref, o_ref, acc_ref):
    g = pl.program_id(1)

    @pl.when(g == 0)
    def _():
        acc_ref[...] = base_ref[...]

    comb = _route(logits_ref[...])
    lane = lax.broadcasted_iota(jnp.int32, comb.shape, 1)
    h = h_ref[...]
    cols = []
    for e in range(EXPERTS_PER_GROUP):
        c = jnp.sum(jnp.where(lane == g * EXPERTS_PER_GROUP + e, comb, 0.0), axis=-1, keepdims=True)
        hid = _silu(_dot(h, wg_ref[e])) * _dot(h, wu_ref[e])
        cols.append((hid * c).astype(BF16))
    hid_all = jnp.concatenate(cols, axis=1)
    wd = wd_ref[...].reshape(EXPERTS_PER_GROUP * EXPERT_FF, D_MODEL)
    acc_ref[...] += _dot(hid_all, wd)

    @pl.when(g == N_GROUPS - 1)
    def _():
        o_ref[...] = _layer_norm(acc_ref[...], g2_ref[...], b2_ref[...])


def _moe(h1b, base, logits, wg, wu, wd, g2, b2, tm):
    n = h1b.shape[0]
    row = lambda i, g: (i, 0)
    return pl.pallas_call(
        _moe_kernel,
        grid=(n // tm, N_GROUPS),
        in_specs=[
            pl.BlockSpec((tm, D_MODEL), row),
            pl.BlockSpec((tm, D_MODEL), row),
            pl.BlockSpec((tm, LANES), row),
            pl.BlockSpec((EXPERTS_PER_GROUP, D_MODEL, EXPERT_FF), lambda i, g: (g, 0, 0)),
            pl.BlockSpec((EXPERTS_PER_GROUP, D_MODEL, EXPERT_FF), lambda i, g: (g, 0, 0)),
            pl.BlockSpec((EXPERTS_PER_GROUP, EXPERT_FF, D_MODEL), lambda i, g: (g, 0, 0)),
            pl.BlockSpec((1, D_MODEL), lambda i, g: (0, 0)),
            pl.BlockSpec((1, D_MODEL), lambda i, g: (0, 0)),
        ],
        out_specs=pl.BlockSpec((tm, D_MODEL), row),
        out_shape=jax.ShapeDtypeStruct((n, D_MODEL), F32),
        scratch_shapes=[pltpu.VMEM((tm, D_MODEL), F32)],
        compiler_params=_cparams(("parallel", "arbitrary")),
        name="moe",
    )(h1b, base, logits, wg, wu, wd, g2, b2)


def _row(v):
    return v.reshape(1, -1).astype(F32)


def _pad_lanes(v, offset=0):
    out = jnp.zeros((1, LANES), F32)
    return lax.dynamic_update_slice(out, v.reshape(1, -1).astype(F32), (0, offset))


def _tile(n, pref):
    t = min(pref, n)
    while n % t:
        t //= 2
    return t


def kernel(x, p, emb_ln_g, emb_ln_b, w_in, b_in, conv_w, dn_a_log, dn_dt_bias, dn_norm_w, w_dn_o,
           da_lq1, da_lk1, da_lq2, da_lk2, da_subln_w, w_da_o, w_out, ln1_g, ln1_b,
           w_router_group, b_router_group, w_router_expert, b_router_expert,
           w_exp_gate, w_exp_up, w_exp_down, w_ple_gate, b_ple_gate, w_ple_proj, ln2_g, ln2_b):
    B, S, _ = x.shape
    n = B * S
    assert S % CHUNK == 0
    li = 0
    offs = np.concatenate([[0], np.cumsum(IN_SIZES)]).astype(int)
    c_dn, c_z, c_b, c_a, c_q, c_k, c_v, c_g = [slice(int(offs[j]), int(offs[j + 1])) for j in range(8)]
    w_i, b_i = w_in[li], b_in[li]

    small_w = jnp.zeros((D_MODEL, LANES), F32)
    small_w = small_w.at[:, 0:DN_HEADS].set(w_i[:, c_b]).at[:, DN_HEADS:2 * DN_HEADS].set(w_i[:, c_a])
    small_b = jnp.zeros((LANES,), F32)
    small_b = small_b.at[0:DN_HEADS].set(b_i[c_b]).at[DN_HEADS:2 * DN_HEADS].set(b_i[c_a])
    w_a = jnp.concatenate([w_i[:, c_dn], small_w, w_i[:, c_q], w_i[:, c_k], w_i[:, c_v]], axis=1).astype(BF16)
    b_a = jnp.concatenate([b_i[c_dn], small_b, b_i[c_q], b_i[c_k], b_i[c_v]]).reshape(1, -1).astype(F32)
    alog = _pad_lanes(dn_a_log[li], DN_HEADS)
    dtb = _pad_lanes(dn_dt_bias[li], DN_HEADS)

    x2 = x.reshape(n, D_MODEL)
    tm_a = _tile(n, 512)
    dn_raw, ba, bat, qk, vt = _in_proj(x2, _row(emb_ln_g), _row(emb_ln_b), w_a, b_a, alog, dtb,
                                  conv_w[li].astype(F32), tm_a, S)

    tb = _tile(S, 1024)
    o_dn = _deltanet(dn_raw.reshape(B, S, DN_COLS), ba.reshape(B, S, LANES), bat, _row(dn_norm_w[li]), tb)

    slopes = jnp.asarray(2.0 ** (-8.0 * np.arange(1, DA_HEADS + 1) / DA_HEADS), dtype=F32)
    slopes = jnp.broadcast_to(slopes[:, None, None], (DA_HEADS, 1, LANES))
    o_da = _diffattn(qk.reshape(B, S, 2 * DA_QK_WIDTH), vt, _row(da_lq1[li]), _row(da_lk1[li]),
                      _row(da_lq2[li]), _row(da_lk2[li]), slopes, _row(da_subln_w[li]))

    wr = jnp.zeros((D_MODEL, LANES), F32)
    wr = wr.at[:, 0:N_EXPERTS].set(w_router_expert[li])
    wr = wr.at[:, ROUTER_GROUP_LANE0:ROUTER_GROUP_LANE0 + N_GROUPS].set(w_router_group[li])
    wr_hi = wr.astype(BF16)
    wr_lo = (wr - wr_hi.astype(F32)).astype(BF16)
    br = jnp.zeros((LANES,), F32).at[0:N_EXPERTS].set(b_router_expert[li])
    br = br.at[ROUTER_GROUP_LANE0:ROUTER_GROUP_LANE0 + N_GROUPS].set(b_router_group[li]).reshape(1, LANES)
    ws = (_row(emb_ln_g), _row(emb_ln_b),
          w_i[:, c_z].astype(BF16), _row(b_i[c_z]),
          w_i[:, c_g].astype(BF16), _row(b_i[c_g]),
          w_dn_o[li].astype(BF16), w_da_o[li].astype(BF16), w_out[li].astype(BF16),
          _row(ln1_g[li]), _row(ln1_b[li]), jnp.concatenate([wr_hi, wr_lo], axis=1), wr_hi, br,
          w_ple_gate[li].astype(BF16), _row(b_ple_gate[li]), w_ple_proj[li].astype(BF16))
    tm_c = _tile(n, 512)
    h1b, base, logits = _mixer(x2, o_dn.reshape(n, DN_WIDTH), o_da.reshape(n, DA_WIDTH),
                             p[li].reshape(n, PLE_DIM), ws, tm_c)

    wg = w_exp_gate[li].astype(BF16)
    wu = w_exp_up[li].astype(BF16)
    wd = w_exp_down[li].astype(BF16)
    tm_d = _tile(n, 1024)
    out = _moe(h1b, base, logits, wg, wu, wd, _row(ln2_g[li]), _row(ln2_b[li]), tm_d)
    return out.reshape(B, S, D_MODEL)
```

```python
import functools
import math

import jax
import jax.numpy as jnp
import numpy as np
from jax import lax
from jax.experimental import pallas as pl
from jax.experimental.pallas import tpu as pltpu

F32 = jnp.float32
BF16 = jnp.bfloat16

D_MODEL = 1024
PLE_DIM = 256
DN_HEADS = 4
DN_HEAD_DIM = 128
DN_WIDTH = DN_HEADS * DN_HEAD_DIM
CONV_WIDTH = 4
CHUNK = 64
DA_HEADS = 4
DA_HEAD_DIM = 64
DA_V_DIM = 2 * DA_HEAD_DIM
DA_QK_WIDTH = DA_HEADS * 2 * DA_HEAD_DIM
DA_WIDTH = DA_HEADS * DA_V_DIM
N_GROUPS = 4
EXPERTS_PER_GROUP = 4
N_EXPERTS = N_GROUPS * EXPERTS_PER_GROUP
EXPERT_FF = 256
IN_SIZES = (3 * DN_WIDTH, DN_WIDTH, DN_HEADS, DN_HEADS, DA_QK_WIDTH, DA_QK_WIDTH, DA_WIDTH, 2 * D_MODEL)
LN_EPS = 1e-5
RMS_EPS = 1e-6
DEPTH = 1
ALPHA = (2.0 * DEPTH) ** 0.25
LAM_INIT = 0.8 - 0.6 * math.exp(-0.3 * 0)

LANES = 128
VMEM_LIMIT = 56 * 1024 * 1024

NEG_INF = float("-inf")


def _cparams(sem, flags=None):
    return pltpu.CompilerParams(dimension_semantics=sem, vmem_limit_bytes=VMEM_LIMIT, flags=flags)


def _layer_norm(x, g, b):
    mu = jnp.mean(x, axis=-1, keepdims=True)
    xc = x - mu
    var = jnp.mean(xc * xc, axis=-1, keepdims=True)
    return xc * lax.rsqrt(var + LN_EPS) * g + b


def _sigmoid(x):
    return 1.0 / (1.0 + jnp.exp2(x * (-LOG2E)))


def _silu(x):
    return x * _sigmoid(x)


def _dot(a, b):
    return jnp.dot(a, b, preferred_element_type=F32)


def _dot_nt(a, b):
    return lax.dot_general(a, b, (((1,), (1,)), ((), ())), preferred_element_type=F32)


DN_COLS = 3 * DN_WIDTH
DA_V_PAD = 16
DA_V_AUG = DA_V_DIM + DA_V_PAD
DA_TK = 512
DA_TQ = DA_TK
LOG2E = 1.4426950408889634


IN_PROJ_ROW_PARTS = 2
TAIL = 8
BAT_ROWS = 8


def _in_proj_kernel(tiles_per_seq, x_ref, g_ref, b_ref, w_ref, bias_ref, alog_ref, dtb_ref, cw_ref,
                    dn_ref, ba_ref, bat_ref, qk_ref, vt_ref, xe_ref):
    tm = x_ref.shape[0]
    assert vt_ref.shape[0] == 1 and tm == DA_TK
    n_parts = IN_PROJ_ROW_PARTS
    pm = tm // n_parts
    parts = [slice(r * pm, (r + 1) * pm) for r in range(n_parts)]
    c0 = DN_COLS + LANES
    qw = DA_QK_WIDTH

    def proj(hb, lo, hi):
        return _dot(hb, w_ref[:, lo:hi]) + bias_ref[:, lo:hi]

    @pl.when(pl.program_id(0) % tiles_per_seq == 0)
    def _():
        xe_ref[tm:tm + TAIL, :] = jnp.zeros((TAIL, DN_COLS), F32)

    xe_ref[0:TAIL, :] = xe_ref[tm:tm + TAIL, :]
    hbs = [_layer_norm(x_ref[rows, :], g_ref[...], b_ref[...]).astype(BF16) for rows in parts]
    for hb, rows in zip(hbs, parts):
        xe_ref[TAIL + rows.start:TAIL + rows.stop, :] = proj(hb, 0, DN_COLS)
    for hb, rows in zip(hbs, parts):
        qk_ref[rows, :qw] = (proj(hb, c0, c0 + qw) * (DA_HEAD_DIM ** -0.5 * LOG2E)).astype(BF16)
        qk_ref[rows, qw:] = proj(hb, c0 + qw, c0 + 2 * qw).astype(BF16)
    for hb, rows in zip(hbs, parts):
        v_t = proj(hb, c0 + 2 * qw, c0 + 3 * qw).T.astype(BF16)
        extra = (lax.broadcasted_iota(jnp.int32, (DA_V_PAD, pm), 0) == 0).astype(F32).astype(BF16)
        vt_ref[0, :, rows] = jnp.concatenate(
            [blk for hd in range(DA_HEADS) for blk in (v_t[hd * DA_V_DIM:(hd + 1) * DA_V_DIM], extra)], axis=0)
    for hb, rows in zip(hbs, parts):
        ba = proj(hb, DN_COLS, DN_COLS + LANES)
        lane = lax.broadcasted_iota(jnp.int32, (pm, LANES), 1)
        row = lax.broadcasted_iota(jnp.int32, (pm, LANES), 0)
        beta = _sigmoid(ba)
        xs = ba + dtb_ref[...]
        softplus = jnp.maximum(xs, 0.0) + jnp.log(1.0 + jnp.exp(-jnp.abs(xs)))
        g = -jnp.exp(alog_ref[...]) * softplus
        g = jnp.where((lane >= DN_HEADS) & (lane < 2 * DN_HEADS), g, 0.0)
        rin = row & (CHUNK - 1)
        k = 1
        while k < CHUNK:
            g = g + jnp.where(rin >= k, pltpu.roll(g, k, 0), 0.0)
            k *= 2
        ba_out = jnp.where(lane < DN_HEADS, beta, g)
        ba_ref[rows, :] = ba_out
        bat_ref[:, rows] = ba_out.T[0:BAT_ROWS, :]

    cw = cw_ref[...]
    hd = DN_HEAD_DIM
    for rows in parts:
        for hh in range(3 * DN_HEADS):
            cols = slice(hh * hd, (hh + 1) * hd)
            r0 = TAIL + rows.start
            xh = xe_ref[r0:r0 + pm, cols] * cw[CONV_WIDTH - 1:CONV_WIDTH, cols]
            for j in range(CONV_WIDTH - 1):
                off = r0 - (CONV_WIDTH - 1) + j
                xh = xh + xe_ref[off:off + pm, cols] * cw[j:j + 1, cols]
            xh = _silu(xh)
            if hh < 2 * DN_HEADS:
                scale = (hd ** -0.5) if hh < DN_HEADS else 1.0
                xh = xh * lax.rsqrt(jnp.sum(xh * xh, axis=-1, keepdims=True) + RMS_EPS) * scale
            dn_ref[rows, cols] = xh.astype(dn_ref.dtype)


def _in_proj(x2, emb_g, emb_b, w_a, b_a, alog, dtb, conv_w, tm, seq):
    n = x2.shape[0]
    wcols = w_a.shape[1]
    assert seq % tm == 0
    const = lambda i: (0, 0)
    return pl.pallas_call(
        functools.partial(_in_proj_kernel, seq // tm),
        grid=(n // tm,),
        in_specs=[
            pl.BlockSpec((tm, D_MODEL), lambda i: (i, 0)),
            pl.BlockSpec((1, D_MODEL), const),
            pl.BlockSpec((1, D_MODEL), const),
            pl.BlockSpec((D_MODEL, wcols), const),
            pl.BlockSpec((1, wcols), const),
            pl.BlockSpec((1, LANES), const),
            pl.BlockSpec((1, LANES), const),
            pl.BlockSpec((CONV_WIDTH, DN_COLS), const),
        ],
        out_specs=[
            pl.BlockSpec((tm, DN_COLS), lambda i: (i, 0)),
            pl.BlockSpec((tm, LANES), lambda i: (i, 0)),
            pl.BlockSpec((BAT_ROWS, tm), lambda i: (0, i)),
            pl.BlockSpec((tm, 2 * DA_QK_WIDTH), lambda i: (i, 0)),
            pl.BlockSpec((tm // DA_TK, DA_HEADS * DA_V_AUG, DA_TK), lambda i: (i, 0, 0)),
        ],
        out_shape=[
            jax.ShapeDtypeStruct((n, DN_COLS), BF16),
            jax.ShapeDtypeStruct((n, LANES), F32),
            jax.ShapeDtypeStruct((BAT_ROWS, n), F32),
            jax.ShapeDtypeStruct((n, 2 * DA_QK_WIDTH), BF16),
            jax.ShapeDtypeStruct((n // DA_TK, DA_HEADS * DA_V_AUG, DA_TK), BF16),
        ],
        scratch_shapes=[pltpu.VMEM((tm + 2 * TAIL, DN_COLS), F32)],
        compiler_params=_cparams(("arbitrary",)),
        name="in_proj",
    )(x2, emb_g, emb_b, w_a, b_a, alog, dtb, conv_w)


DN_GROUP_CHUNKS = 4


def _dn_kernel(x_ref, ba_ref, gt_ref, nw_ref, o_ref, state_ref):
    sblk = pl.program_id(1)
    tb = x_ref.shape[0]
    nck = tb // CHUNK
    hh = DN_HEADS
    hd = DN_HEAD_DIM

    @pl.when(sblk == 0)
    def _():
        state_ref[...] = jnp.zeros_like(state_ref)

    ba = ba_ref[...]
    ri = lax.broadcasted_iota(jnp.int32, (CHUNK, CHUNK), 0)
    ci = lax.broadcasted_iota(jnp.int32, (CHUNK, CHUNK), 1)
    causal = ri >= ci
    strict = ri > ci
    eye = jnp.where(ri == ci, 1.0, 0.0).astype(F32)
    nw = nw_ref[...]

    q_b, k_b, kb_b, rhs_uw, qg_b, kgt_b, decay, eg_last = {}, {}, {}, {}, {}, {}, {}, {}
    for h in range(hh):
        q = x_ref[:, h * hd:(h + 1) * hd].astype(F32)
        k = x_ref[:, (hh + h) * hd:(hh + h + 1) * hd].astype(F32)
        v = x_ref[:, (2 * hh + h) * hd:(2 * hh + h + 1) * hd].astype(F32)
        bcol = ba[:, h:h + 1]
        gcol = ba[:, hh + h:hh + h + 1]
        grow = gt_ref[hh + h:hh + h + 1, :]
        egcol = jnp.exp(gcol)
        kb = k * bcol
        uw = jnp.concatenate([v * bcol, kb * egcol], axis=1).astype(BF16)
        qg = (q * egcol).astype(BF16)
        qb = q.astype(BF16)
        kbb = kb.astype(BF16)
        kk = k.astype(BF16)
        for c in range(nck):
            sl = slice(c * CHUNK, (c + 1) * CHUNK)
            gc = gcol[sl]
            g_last = gc[CHUNK - 1:CHUNK, :]
            q_b[h, c], k_b[h, c], kb_b[h, c] = qb[sl], kk[sl], kbb[sl]
            rhs_uw[h, c], qg_b[h, c] = uw[sl], qg[sl]
            kgt_b[h, c] = (k[sl] * jnp.exp(g_last - gc)).T.astype(BF16)
            decay[h, c] = jnp.exp(jnp.where(causal, gc - grow[:, sl], NEG_INF))
            eg_last[h, c] = jnp.exp(g_last)

    power, t, a_intra, u, w_b = {}, {}, {}, {}, {}
    n_rounds = int(math.log2(CHUNK)) - 1

    def local_stages(group):
        def s_nmat():
            for p_ in group:
                nmat = jnp.where(strict, -_dot_nt(kb_b[p_], k_b[p_]) * decay[p_], 0.0)
                power[p_] = nmat
                t[p_] = eye + nmat

        def s_intra():
            for p_ in group:
                a_intra[p_] = (_dot_nt(q_b[p_], k_b[p_]) * decay[p_]).astype(BF16)

        def s_square():
            for p_ in group:
                pb = power[p_].astype(BF16)
                power[p_] = _dot(pb, pb)

        def s_round(last):
            for p_ in group:
                pb = power[p_].astype(BF16)
                if last:
                    t[p_] = t[p_] + _dot(t[p_].astype(BF16), pb)
                else:
                    both = _dot(jnp.concatenate([power[p_], t[p_]], axis=0).astype(BF16), pb)
                    power[p_] = both[:CHUNK]
                    t[p_] = t[p_] + both[CHUNK:]

        def s_uw():
            for p_ in group:
                uw = _dot(t[p_].astype(BF16), rhs_uw[p_])
                u[p_] = uw[:, :hd]
                w_b[p_] = uw[:, hd:].astype(BF16)

        rounds = [functools.partial(s_round, r == n_rounds - 1) for r in range(n_rounds)]
        return [s_nmat, s_intra, s_square] + rounds + [s_uw]

    states = [state_ref[h] for h in range(hh)]
    ws = {}

    def recur_stages(chunks):
        def r_first(c):
            for h in range(hh):
                lhs = jnp.concatenate([w_b[h, c], qg_b[h, c]], axis=0)
                ws[h] = _dot(lhs, states[h].astype(BF16))

        def r_second(c):
            for h in range(hh):
                v_new = (u[h, c] - ws[h][:CHUNK]).astype(BF16)
                lhs = jnp.concatenate([a_intra[h, c], kgt_b[h, c]], axis=0)
                both = _dot(lhs, v_new)
                o_c = ws[h][CHUNK:] + both[:CHUNK]
                states[h] = states[h] * eg_last[h, c] + both[CHUNK:]
                o_n = o_c * lax.rsqrt(jnp.mean(o_c * o_c, axis=-1, keepdims=True) + RMS_EPS) * nw
                o_ref[c * CHUNK:(c + 1) * CHUNK, h * hd:(h + 1) * hd] = o_n

        return [functools.partial(f, c) for c in chunks for f in (r_first, r_second)]

    gsz = min(DN_GROUP_CHUNKS, nck)
    groups = [list(range(g0, g0 + gsz)) for g0 in range(0, nck, gsz)]
    pending = []
    for chunks in groups:
        local = local_stages([(h, c) for c in chunks for h in range(hh)])
        for k, stage in enumerate(local):
            stage()
            if pending:
                pending.pop(0)()
        while pending:
            pending.pop(0)()
        pending = recur_stages(chunks)
    while pending:
        pending.pop(0)()
    for h in range(hh):
        state_ref[h] = states[h]


def _deltanet(dn3, ba3, bat, norm_w, tb):
    b, s, _ = dn3.shape
    hh = DN_HEADS
    nblk = s // tb
    return pl.pallas_call(
        _dn_kernel,
        grid=(b, s // tb),
        in_specs=[
            pl.BlockSpec((None, tb, DN_COLS), lambda bi, i: (bi, i, 0)),
            pl.BlockSpec((None, tb, LANES), lambda bi, i: (bi, i, 0)),
            pl.BlockSpec((BAT_ROWS, tb), lambda bi, i: (0, bi * nblk + i)),
            pl.BlockSpec((1, LANES), lambda bi, i: (0, 0)),
        ],
        out_specs=pl.BlockSpec((None, tb, DN_WIDTH), lambda bi, i: (bi, i, 0)),
        out_shape=jax.ShapeDtypeStruct((b, s, DN_WIDTH), F32),
        scratch_shapes=[pltpu.VMEM((hh, DN_HEAD_DIM, DN_HEAD_DIM), F32)],
        compiler_params=_cparams(("parallel", "arbitrary")),
        name="deltanet",
    )(dn3, ba3, bat, norm_w)


N_BIAS_LANES = 3
DA_UNROLL = 2


def _da_kernel(lq1_ref, lk1_ref, lq2_ref, lk2_ref, slope_ref, q_ref, k_ref, vt_ref, w_ref, o_ref,
                ka1_ref, ka2_ref, *scr):
    i = pl.program_id(2)
    tq = q_ref.shape[0]
    s_refs = ((scr[0], scr[1]), (scr[2], scr[3]))
    p_refs = ((scr[4], scr[5]), (scr[6], scr[7]))
    acc_refs = (scr[8], scr[9])
    st_ref = scr[10]
    kn_ref = scr[11]
    feat_ref = scr[12]
    nkt, tk, _ = ka1_ref.shape
    half = DA_HEAD_DIM

    @pl.when((pl.program_id(1) == 0) & (i == 0))
    def _():
        c = slope_ref[...] * LOG2E
        lane = lax.broadcasted_iota(jnp.int32, (tk, LANES), 1)
        row = lax.broadcasted_iota(jnp.int32, (tk, LANES), 0)

        def build_feat(t, carry):
            bias = c * (t * tk + row).astype(F32)
            hi = bias.astype(BF16).astype(F32)
            rem = bias - hi
            mid = rem.astype(BF16).astype(F32)
            lo = rem - mid
            piece = lane & (half - 1)
            feat = jnp.where(piece == 0, hi, jnp.where(piece == 1, mid, jnp.where(piece == 2, lo, 0.0)))
            feat_ref[t] = feat.astype(BF16)
            return carry

        lax.fori_loop(0, nkt, build_feat, 0)

    @pl.when(i == 0)
    def _():
        lane = lax.broadcasted_iota(jnp.int32, (tk, LANES), 1)

        def build(t, carry):
            kb = k_ref[pl.ds(pl.multiple_of(t * tk, tk), tk), :].astype(F32)
            feat = feat_ref[t].astype(F32)
            ka1_ref[t] = jnp.where(lane < half, kb, feat).astype(BF16)
            ka2_ref[t] = jnp.where(lane >= half, kb, feat).astype(BF16)
            sq = kb * kb
            n1 = jnp.max(jnp.sum(jnp.where(lane < half, sq, 0.0), axis=-1, keepdims=True), axis=0, keepdims=True)
            n2 = jnp.max(jnp.sum(jnp.where(lane >= half, sq, 0.0), axis=-1, keepdims=True), axis=0, keepdims=True)
            kn1, kn2 = carry
            return jnp.where(tile_lane == t, n1, kn1), jnp.where(tile_lane == t, n2, kn2)

        tile_lane = lax.broadcasted_iota(jnp.int32, (1, LANES), 1)
        zeros = jnp.zeros((1, LANES), F32)
        kn1, kn2 = lax.fori_loop(0, nkt, build, (zeros, zeros))
        kn_ref[0:1, :] = kn1
        kn_ref[1:2, :] = kn2

    qv = q_ref[...].astype(F32)
    lane_q = lax.broadcasted_iota(jnp.int32, qv.shape, 1)
    q1 = jnp.where(lane_q < half, qv, jnp.where(lane_q < half + N_BIAS_LANES, 1.0, 0.0))
    q2 = jnp.where(lane_q >= half, qv, jnp.where(lane_q < N_BIAS_LANES, 1.0, 0.0))
    qts = (q1.T.astype(BF16), q2.T.astype(BF16))
    kas = (ka1_ref, ka2_ref)

    SKIP_BELOW = -170.0
    NORM_SLACK = 1.01
    qsq = qv * qv
    qn = [jnp.sqrt(jnp.max(jnp.sum(jnp.where(msk, qsq, 0.0), axis=-1, keepdims=True), axis=0, keepdims=True))
          for msk in (lane_q < half, lane_q >= half)]
    tile_lane = lax.broadcasted_iota(jnp.int32, (1, LANES), 1)
    gap = ((i - tile_lane - 1) * tk + 1).astype(F32)
    c_row = slope_ref[...] * LOG2E
    bound = None
    for mp in range(2):
        kn = jnp.sqrt(kn_ref[mp:mp + 1, :])
        kd = jnp.max(jnp.where(tile_lane == i, kn, 0.0), axis=-1, keepdims=True)
        b_mp = qn[mp] * (kn + kd) * NORM_SLACK - c_row * gap
        bound = b_mp if bound is None else jnp.maximum(bound, b_mp)
    skippable = (bound < SKIP_BELOW) & (tile_lane < i)
    first_keep = jnp.min(jnp.where(skippable, float(LANES), tile_lane.astype(F32)), axis=-1, keepdims=True)
    base = first_keep.astype(jnp.int32)[0, 0]
    cnt = i - base

    M_ROW, A_ROW = 0, 2
    CH = 128
    SUB = 8
    V_AFTER_CHUNK = DA_TK // CH - 1

    def row(mp, r):
        return st_ref[mp, r:r + 1, :]

    def fold(x, op):
        return op(x.reshape(CH // SUB, SUB, tq), axis=0)

    def step(t, slot, s_mode, do_v=True, do_e=True):
        o = 1 - slot
        a_prevs = [row(mp, A_ROW + o) for mp in range(2)] if do_v else None
        for mp in range(2):
            m_t = row(mp, M_ROW + slot)
            tmax = jnp.full((SUB, tq), NEG_INF, F32)
            for c in range(tk // CH):
                rows = slice(c * CH, (c + 1) * CH)
                if do_e:
                    pc = jnp.exp2(s_refs[slot][mp][rows, :] - m_t)
                    p_refs[slot][mp][rows, :] = pc.astype(BF16)
                if s_mode is not None:
                    sc = _dot(kas[mp][base + t + 1, rows, :], qts[mp])
                    if s_mode != "plain":
                        krow = lax.broadcasted_iota(jnp.int32, (CH, tq), 0) + (c * CH + s_mode)
                        qcol = lax.broadcasted_iota(jnp.int32, (CH, tq), 1)
                        sc = jnp.where(krow <= qcol, sc, NEG_INF)
                    s_refs[o][mp][rows, :] = sc
                    tmax = jnp.maximum(tmax, fold(sc, jnp.max))
                if do_v and c == V_AFTER_CHUNK:
                    vt = vt_ref[base + t - 1]
                    acc_refs[mp][...] = a_prevs[mp] * acc_refs[mp][...] + _dot(vt, p_refs[o][mp][...])
            if s_mode is not None:
                m_new = jnp.maximum(m_t, jnp.max(tmax, axis=0, keepdims=True))
                st_ref[mp, M_ROW + o:M_ROW + o + 1, :] = m_new
                st_ref[mp, A_ROW + o:A_ROW + o + 1, :] = jnp.exp2(m_t - m_new)

    for mp in range(2):
        acc_refs[mp][...] = jnp.zeros((DA_V_AUG, tq), F32)
        st_ref[mp, M_ROW + 1:M_ROW + 2, :] = jnp.full((1, tq), -1e30, F32)

    def finish():
        l1 = acc_refs[0][DA_V_DIM:DA_V_DIM + 1, :]
        l2 = acc_refs[1][DA_V_DIM:DA_V_DIM + 1, :]
        a1 = acc_refs[0][0:DA_V_DIM, :]
        a2 = acc_refs[1][0:DA_V_DIM, :]
        lam = (jnp.exp(jnp.sum(lq1_ref[...] * lk1_ref[...], axis=-1, keepdims=True))
               - jnp.exp(jnp.sum(lq2_ref[...] * lk2_ref[...], axis=-1, keepdims=True)) + LAM_INIT)
        o_t = a1 / l1 - lam * (a2 / l2)
        o_t = o_t * lax.rsqrt(jnp.mean(o_t * o_t, axis=0, keepdims=True) + RMS_EPS) * (1.0 - LAM_INIT)
        o_ref[...] = (o_t.T * w_ref[...]).astype(BF16)

    @pl.when(cnt == 0)
    def _():
        step(-1, 1, 0, do_v=False, do_e=False)
        step(0, 0, None, do_v=False)
        step(1, 1, None, do_e=False)
        finish()

    @pl.when(cnt == 1)
    def _():
        step(-1, 1, "plain", do_v=False, do_e=False)
        step(0, 0, 0, do_v=False)
        step(1, 1, None)
        step(2, 0, None, do_e=False)
        finish()

    @pl.when(cnt >= 2)
    def _():
        step(-1, 1, "plain", do_v=False, do_e=False)
        step(0, 0, "plain", do_v=False)

    unroll = DA_UNROLL

    def body(it, carry):
        for k in range(unroll):
            step(1 + unroll * it + k, (1 + k) & 1, "plain")
        return carry

    n_plain = jnp.maximum(cnt - 2, 0)
    lax.fori_loop(0, n_plain // unroll, body, 0)

    for rem in range(unroll):
        @pl.when((cnt >= 2) & (n_plain % unroll == rem))
        def _(rem=rem):
            for k in range(rem):
                step(cnt - 1 - rem + k, (1 + k) & 1, "plain")
            step(cnt - 1, (1 + rem) & 1, 0)
            step(cnt, rem & 1, None)
            step(cnt + 1, (1 + rem) & 1, None, do_e=False)
            finish()


def _diffattn(qk3, vt3, lq1, lk1, lq2, lk2, slopes, subln_w):
    b, s, _ = qk3.shape
    hh = DA_HEADS
    tq, tk = DA_TQ, DA_TK
    assert tq == tk and s % tq == 0
    nkt = s // tk
    c2 = lambda h, bi, i: (0, 0)
    return pl.pallas_call(
        _da_kernel,
        grid=(hh, b, s // tq),
        in_specs=[
            pl.BlockSpec((1, DA_HEAD_DIM), c2),
            pl.BlockSpec((1, DA_HEAD_DIM), c2),
            pl.BlockSpec((1, DA_HEAD_DIM), c2),
            pl.BlockSpec((1, DA_HEAD_DIM), c2),
            pl.BlockSpec((None, 1, LANES), lambda h, bi, i: (h, 0, 0)),
            pl.BlockSpec((None, tq, LANES), lambda h, bi, i: (bi, i, h)),
            pl.BlockSpec((None, s, LANES), lambda h, bi, i: (bi, 0, hh + h)),
            pl.BlockSpec((nkt, DA_V_AUG, tk), lambda h, bi, i: (bi, h, 0)),
            pl.BlockSpec((1, DA_V_DIM), c2),
        ],
        out_specs=pl.BlockSpec((None, tq, LANES), lambda h, bi, i: (bi, i, h)),
        out_shape=jax.ShapeDtypeStruct((b, s, DA_WIDTH), BF16),
        scratch_shapes=[
            pltpu.VMEM((nkt, tk, LANES), BF16),
            pltpu.VMEM((nkt, tk, LANES), BF16),
        ] + [pltpu.VMEM((tk, tq), F32)] * 4 + [pltpu.VMEM((tk, tq), BF16)] * 4 + [
            pltpu.VMEM((DA_V_AUG, tq), F32),
            pltpu.VMEM((DA_V_AUG, tq), F32),
            pltpu.VMEM((2, 8, tq), F32),
            pltpu.VMEM((8, LANES), F32),
            pltpu.VMEM((nkt, tk, LANES), BF16),
        ],
        compiler_params=_cparams(("arbitrary", "arbitrary", "arbitrary")),
        name="diffattn",
    )(lq1, lk1, lq2, lk2, slopes, qk3, qk3, vt3, subln_w)


ROUTER_GROUP_LANE0 = N_EXPERTS
MIXER_ROW_PARTS = 2


def _mixer_kernel(x_ref, odn_ref, oda_ref, p_ref, eg_ref, eb_ref, wz_ref, bz_ref, wg_ref, bg_ref,
                  wdn_ref, wda_ref, wout_ref, g1_ref, b1_ref, wr_cat_ref, wr_hi_ref, br_ref,
                  wpg_ref, bpg_ref, wpp_ref, h1b_ref, base_ref, logits_ref):
    n_parts = MIXER_ROW_PARTS
    pm = x_ref.shape[0] // n_parts
    parts = [slice(r * pm, (r + 1) * pm) for r in range(n_parts)]
    st = [dict() for _ in parts]
    for s, rows in zip(st, parts):
        s["h"] = _layer_norm(x_ref[rows, :], eg_ref[...], eb_ref[...])
        s["hb"] = s["h"].astype(BF16)
    for s, rows in zip(st, parts):
        s["z"] = _dot(s["hb"], wz_ref[...]) + bz_ref[...]
        s["gl"] = _dot(s["hb"], wg_ref[...]) + bg_ref[...]
    for s, rows in zip(st, parts):
        s["y_dn"] = _dot((odn_ref[rows, :] * _silu(s["z"])).astype(BF16), wdn_ref[...])
        s["y_da"] = _dot(oda_ref[rows, :], wda_ref[...])
    for s, rows in zip(st, parts):
        gates = _sigmoid(s["gl"])
        mixed = gates[:, :D_MODEL] * s["y_dn"] + gates[:, D_MODEL:] * s["y_da"]
        s["mix"] = _dot(mixed.astype(BF16), wout_ref[...])
    for s, rows in zip(st, parts):
        s["h1"] = _layer_norm(ALPHA * s["h"] + s["mix"], g1_ref[...], b1_ref[...])
        s["h1b"] = s["h1"].astype(BF16)
        h1b_ref[rows, :] = s["h1b"]
    for s, rows in zip(st, parts):
        ple = (_sigmoid(_dot(s["h1b"], wpg_ref[...]) + bpg_ref[...])
               * _dot(p_ref[rows, :].astype(BF16), wpp_ref[...]))
        base_ref[rows, :] = ALPHA * s["h1"] + ple
        h1_lo = (s["h1"] - s["h1b"].astype(F32)).astype(BF16)
        both = _dot(s["h1b"], wr_cat_ref[...])
        s["logits"] = (both[:, :LANES] + both[:, LANES:] + _dot(h1_lo, wr_hi_ref[...]) + br_ref[...])
    for s, rows in zip(st, parts):
        logits_ref[rows, :] = s["logits"]


def _route(logits):
    tm = logits.shape[0]
    lane = lax.broadcasted_iota(jnp.int32, (tm, LANES), 1).astype(F32)
    big = float(LANES)
    is_g = (lane >= ROUTER_GROUP_LANE0) & (lane < ROUTER_GROUP_LANE0 + N_GROUPS)
    gl = jnp.where(is_g, logits, NEG_INF)
    gmax = jnp.max(gl, axis=-1, keepdims=True)
    gsum = jnp.sum(jnp.exp(gl - gmax), axis=-1, keepdims=True)
    g_val = 1.0 / gsum
    g_idx = jnp.min(jnp.where(gl == gmax, lane, big), axis=-1, keepdims=True) - ROUTER_GROUP_LANE0
    e_lo = g_idx * EXPERTS_PER_GROUP
    in_g = (lane >= e_lo) & (lane < e_lo + EXPERTS_PER_GROUP)
    el = jnp.where(in_g, logits, NEG_INF)
    emax = jnp.max(el, axis=-1, keepdims=True)
    ee = jnp.exp(el - emax)
    e_prob = ee / jnp.sum(ee, axis=-1, keepdims=True)
    p1 = jnp.max(e_prob, axis=-1, keepdims=True)
    i1 = jnp.min(jnp.where(in_g & (e_prob == p1), lane, big), axis=-1, keepdims=True)
    rest = jnp.where(in_g & (lane != i1), e_prob, -1.0)
    p2 = jnp.max(rest, axis=-1, keepdims=True)
    i2 = jnp.min(jnp.where(rest == p2, lane, big), axis=-1, keepdims=True)
    denom = p1 + p2
    return (jnp.where(lane == i1, p1 / denom * g_val, 0.0)
            + jnp.where(lane == i2, p2 / denom * g_val, 0.0))


def _mixer(x2, odn2, oda2, p2, ws, tm):
    n = x2.shape[0]
    row = lambda i: (i, 0)
    const = lambda i: (0, 0)

    def cspec(a):
        return pl.BlockSpec(a.shape, const, pipeline_mode=pl.Buffered(1))

    (eg, eb, wz, bz, wg, bg, wdn, wda, wout, g1, b1, wr_cat, wr_hi, br, wpg, bpg, wpp) = ws
    return pl.pallas_call(
        _mixer_kernel,
        grid=(n // tm,),
        in_specs=[
            pl.BlockSpec((tm, D_MODEL), row),
            pl.BlockSpec((tm, DN_WIDTH), row),
            pl.BlockSpec((tm, DA_WIDTH), row),
            pl.BlockSpec((tm, PLE_DIM), row),
        ] + [cspec(a) for a in ws],
        out_specs=[
            pl.BlockSpec((tm, D_MODEL), row),
            pl.BlockSpec((tm, D_MODEL), row),
            pl.BlockSpec((tm, LANES), row),
        ],
        out_shape=[
            jax.ShapeDtypeStruct((n, D_MODEL), BF16),
            jax.ShapeDtypeStruct((n, D_MODEL), F32),
            jax.ShapeDtypeStruct((n, LANES), F32),
        ],
        compiler_params=_cparams(("parallel",)),
        name="mixer",
    )(x2, odn2, oda2, p2, *ws)


def _moe_kernel(h_ref, base_ref, logits_ref, wg_ref, wu_ref, wd_ref, g2_ref, b2_ref, o_ref, acc_ref):
    g = pl.program_id(1)

    @pl.when(g == 0)
    def _():
        acc_ref[...] = base_ref[...]

    comb = _route(logits_ref[...])
    lane = lax.broadcasted_iota(jnp.int32, comb.shape, 1)
    h = h_ref[...]
    cols = []
    for e in range(EXPERTS_PER_GROUP):
        c = jnp.sum(jnp.where(lane == g * EXPERTS_PER_GROUP + e, comb, 0.0), axis=-1, keepdims=True)
        hid = _silu(_dot(h, wg_ref[e])) * _dot(h, wu_ref[e])
        cols.append((hid * c).astype(BF16))
    hid_all = jnp.concatenate(cols, axis=1)
    wd = wd_ref[...].reshape(EXPERTS_PER_GROUP * EXPERT_FF, D_MODEL)
    acc_ref[...] += _dot(hid_all, wd)

    @pl.when(g == N_GROUPS - 1)
    def _():
        o_ref[...] = _layer_norm(acc_ref[...], g2_ref[...], b2_ref[...])


def _moe(h1b, base, logits, wg, wu, wd, g2, b2, tm):
    n = h1b.shape[0]
    row = lambda i, g: (i, 0)
    return pl.pallas_call(
        _moe_kernel,
        grid=(n // tm, N_GROUPS),
        in_specs=[
            pl.BlockSpec((tm, D_MODEL), row),
            pl.BlockSpec((tm, D_MODEL), row),
            pl.BlockSpec((tm, LANES), row),
            pl.BlockSpec((EXPERTS_PER_GROUP, D_MODEL, EXPERT_FF), lambda i, g: (g, 0, 0)),
            pl.BlockSpec((EXPERTS_PER_GROUP, D_MODEL, EXPERT_FF), lambda i, g: (g, 0, 0)),
            pl.BlockSpec((EXPERTS_PER_GROUP, EXPERT_FF, D_MODEL), lambda i, g: (g, 0, 0)),
            pl.BlockSpec((1, D_MODEL), lambda i, g: (0, 0)),
            pl.BlockSpec((1, D_MODEL), lambda i, g: (0, 0)),
        ],
        out_specs=pl.BlockSpec((tm, D_MODEL), row),
        out_shape=jax.ShapeDtypeStruct((n, D_MODEL), F32),
        scratch_shapes=[pltpu.VMEM((tm, D_MODEL), F32)],
        compiler_params=_cparams(("parallel", "arbitrary")),
        name="moe",
    )(h1b, base, logits, wg, wu, wd, g2, b2)


def _row(v):
    return v.reshape(1, -1).astype(F32)


def _pad_lanes(v, offset=0):
    out = jnp.zeros((1, LANES), F32)
    return lax.dynamic_update_slice(out, v.reshape(1, -1).astype(F32), (0, offset))


def _tile(n, pref):
    t = min(pref, n)
    while n % t:
        t //= 2
    return t


def kernel(x, p, emb_ln_g, emb_ln_b, w_in, b_in, conv_w, dn_a_log, dn_dt_bias, dn_norm_w, w_dn_o,
           da_lq1, da_lk1, da_lq2, da_lk2, da_subln_w, w_da_o, w_out, ln1_g, ln1_b,
           w_router_group, b_router_group, w_router_expert, b_router_expert,
           w_exp_gate, w_exp_up, w_exp_down, w_ple_gate, b_ple_gate, w_ple_proj, ln2_g, ln2_b):
    B, S, _ = x.shape
    n = B * S
    assert S % CHUNK == 0
    li = 0
    offs = np.concatenate([[0], np.cumsum(IN_SIZES)]).astype(int)
    c_dn, c_z, c_b, c_a, c_q, c_k, c_v, c_g = [slice(int(offs[j]), int(offs[j + 1])) for j in range(8)]
    w_i, b_i = w_in[li], b_in[li]

    small_w = jnp.zeros((D_MODEL, LANES), F32)
    small_w = small_w.at[:, 0:DN_HEADS].set(w_i[:, c_b]).at[:, DN_HEADS:2 * DN_HEADS].set(w_i[:, c_a])
    small_b = jnp.zeros((LANES,), F32)
    small_b = small_b.at[0:DN_HEADS].set(b_i[c_b]).at[DN_HEADS:2 * DN_HEADS].set(b_i[c_a])
    w_a = jnp.concatenate([w_i[:, c_dn], small_w, w_i[:, c_q], w_i[:, c_k], w_i[:, c_v]], axis=1).astype(BF16)
    b_a = jnp.concatenate([b_i[c_dn], small_b, b_i[c_q], b_i[c_k], b_i[c_v]]).reshape(1, -1).astype(F32)
    alog = _pad_lanes(dn_a_log[li], DN_HEADS)
    dtb = _pad_lanes(dn_dt_bias[li], DN_HEADS)

    x2 = x.reshape(n, D_MODEL)
    tm_a = _tile(n, 512)
    dn_raw, ba, bat, qk, vt = _in_proj(x2, _row(emb_ln_g), _row(emb_ln_b), w_a, b_a, alog, dtb,
                                  conv_w[li].astype(F32), tm_a, S)

    tb = _tile(S, 1024)
    o_dn = _deltanet(dn_raw.reshape(B, S, DN_COLS), ba.reshape(B, S, LANES), bat, _row(dn_norm_w[li]), tb)

    slopes = jnp.asarray(2.0 ** (-8.0 * np.arange(1, DA_HEADS + 1) / DA_HEADS), dtype=F32)
    slopes = jnp.broadcast_to(slopes[:, None, None], (DA_HEADS, 1, LANES))
    o_da = _diffattn(qk.reshape(B, S, 2 * DA_QK_WIDTH), vt, _row(da_lq1[li]), _row(da_lk1[li]),
                      _row(da_lq2[li]), _row(da_lk2[li]), slopes, _row(da_subln_w[li]))

    wr = jnp.zeros((D_MODEL, LANES), F32)
    wr = wr.at[:, 0:N_EXPERTS].set(w_router_expert[li])
    wr = wr.at[:, ROUTER_GROUP_LANE0:ROUTER_GROUP_LANE0 + N_GROUPS].set(w_router_group[li])
    wr_hi = wr.astype(BF16)
    wr_lo = (wr - wr_hi.astype(F32)).astype(BF16)
    br = jnp.zeros((LANES,), F32).at[0:N_EXPERTS].set(b_router_expert[li])
    br = br.at[ROUTER_GROUP_LANE0:ROUTER_GROUP_LANE0 + N_GROUPS].set(b_router_group[li]).reshape(1, LANES)
    ws = (_row(emb_ln_g), _row(emb_ln_b),
          w_i[:, c_z].astype(BF16), _row(b_i[c_z]),
          w_i[:, c_g].astype(BF16), _row(b_i[c_g]),
          w_dn_o[li].astype(BF16), w_da_o[li].astype(BF16), w_out[li].astype(BF16),
          _row(ln1_g[li]), _row(ln1_b[li]), jnp.concatenate([wr_hi, wr_lo], axis=1), wr_hi, br,
          w_ple_gate[li].astype(BF16), _row(b_ple_gate[li]), w_ple_proj[li].astype(BF16))
    tm_c = _tile(n, 512)
    h1b, base, logits = _mixer(x2, o_dn.reshape(n, DN_WIDTH), o_da.reshape(n, DA_WIDTH),
                             p[li].reshape(n, PLE_DIM), ws, tm_c)

    wg = w_exp_gate[li].astype(BF16)
    wu = w_exp_up[li].astype(BF16)
    wd = w_exp_down[li].astype(BF16)
    tm_d = _tile(n, 1024)
    out = _moe(h1b, base, logits, wg, wu, wd, _row(ln2_g[li]), _row(ln2_b[li]), tm_d)
    return out.reshape(B, S, D_MODEL)
```
